```python
import jax, jax.numpy as jnp
from jax import lax
import numpy as np

D_MODEL = 1024
BATCH = 8
SEQ = 4096
DEPTH = 4

N_MEM = 256
N_MIXERS = 2
N_GLA = (DEPTH + 1) // 2
N_CONV = DEPTH // 2
GLA_HEADS = 4
GLA_DK = 128
GLA_DV = 256
GLA_QK = GLA_HEADS * GLA_DK
GLA_V = GLA_HEADS * GLA_DV
GLA_GATE_RANK = 16
GLA_GATE_NORM = 16.0
GLA_CHUNK = 64
CONV_CH = D_MODEL
CONV_WIDTH = 31
MEM_HEADS = 4
MEM_DH = 128
MEM_W = MEM_HEADS * MEM_DH
D_FF = 2816
ALPHA = (2 * DEPTH) ** 0.25
BETA = (8 * DEPTH) ** -0.25
LN_EPS = 1e-5
RMS_EPS = 1e-6

GLA_IN = 2 * GLA_QK + 2 * GLA_V + GLA_GATE_RANK + MEM_W
GLA_OUT_IN = GLA_V + MEM_W
CONV_IN = 2 * CONV_CH + MEM_W
CONV_OUT_IN = CONV_CH + MEM_W

kernel_name = "hybrid_gla_conformer_deepnorm_memory"


def layer_norm(x, g, b):
    xf = x.astype(jnp.float32)
    mu = jnp.mean(xf, -1, keepdims=True)
    var = jnp.mean(jnp.square(xf - mu), -1, keepdims=True)
    return ((xf - mu) * lax.rsqrt(var + LN_EPS) * g + b).astype(x.dtype)


def swiglu(x, w_gu, w_down):
    a, b = jnp.split(x @ w_gu, 2, axis=-1)
    return (jax.nn.silu(a) * b) @ w_down


def mem_attention(qm, mem, w_kv):
    B, S, _ = qm.shape
    k, v = jnp.split(mem @ w_kv, 2, axis=-1)
    q = qm.reshape(B, S, MEM_HEADS, MEM_DH)
    k = k.reshape(B, -1, MEM_HEADS, MEM_DH)
    v = v.reshape(B, -1, MEM_HEADS, MEM_DH)
    s = jnp.einsum('bshd,bmhd->bhsm', q, k).astype(jnp.float32) * (MEM_DH ** -0.5)
    p = jax.nn.softmax(s, axis=-1).astype(v.dtype)
    o = jnp.einsum('bhsm,bmhd->bshd', p, v)
    return o.reshape(B, S, MEM_W)


def gla_mix(h, w_gate2, b_gate, g_onorm):
    B, S, _ = h.shape
    N, C = S // GLA_CHUNK, GLA_CHUNK
    splits = [GLA_QK, 2 * GLA_QK, 2 * GLA_QK + GLA_V, 2 * GLA_QK + 2 * GLA_V,
              2 * GLA_QK + 2 * GLA_V + GLA_GATE_RANK]
    q, k, v, r, a, qm = jnp.split(h, splits, axis=-1)
    f32 = jnp.float32
    lg = jax.nn.log_sigmoid((a @ w_gate2 + b_gate).astype(f32)) / GLA_GATE_NORM

    def chunks(t, d):
        return t.astype(f32).reshape(B, N, C, GLA_HEADS, d).transpose(1, 0, 3, 2, 4)

    qc = chunks(q, GLA_DK) * (GLA_DK ** -0.5)
    kc = chunks(k, GLA_DK)
    vc = chunks(v, GLA_DV)
    bcum = jnp.cumsum(chunks(lg, GLA_DK), axis=3)
    b_last = bcum[:, :, :, -1, :]
    qe = qc * jnp.exp(bcum)
    ke = kc * jnp.exp(-bcum)
    kd = kc * jnp.exp(b_last[:, :, :, None, :] - bcum)

    causal = jnp.tril(jnp.ones((C, C), dtype=bool))
    att = jnp.where(causal, jnp.einsum('nbhid,nbhjd->nbhij', qe, ke), 0.0)
    o_intra = jnp.einsum('nbhij,nbhje->nbhie', att, vc)

    def step(state, inp):
        qe_n, kd_n, v_n, bl_n = inp
        o = jnp.einsum('bhcd,bhde->bhce', qe_n, state)
        state = state * jnp.exp(bl_n)[..., None] + jnp.einsum('bhcd,bhce->bhde', kd_n, v_n)
        return state, o

    s0 = jnp.zeros((B, GLA_HEADS, GLA_DK, GLA_DV), f32)
    _, o_inter = lax.scan(step, s0, (qe, kd, vc, b_last))
    o = (o_intra + o_inter).transpose(1, 0, 3, 2, 4).reshape(B, S, GLA_HEADS, GLA_DV)
    o = o * lax.rsqrt(jnp.mean(jnp.square(o), -1, keepdims=True) + RMS_EPS) * g_onorm
    o = o.reshape(B, S, GLA_V) * jax.nn.silu(r.astype(f32))
    return o.astype(h.dtype), qm


def conv_mix(h, w_dw, b_dw, ln_g, ln_b):
    a, gt, qm = jnp.split(h, [CONV_CH, 2 * CONV_CH], axis=-1)
    u = a * jax.nn.sigmoid(gt)
    u = lax.conv_general_dilated(
        u, w_dw[:, None, :], window_strides=(1,), padding=[(CONV_WIDTH - 1, 0)],
        dimension_numbers=('NWC', 'WIO', 'NWC'), feature_group_count=CONV_CH) + b_dw
    u = jax.nn.silu(layer_norm(u, ln_g, ln_b))
    return u, qm


def setup_inputs(seed: int = 0) -> dict:
    key = jax.random.key(seed)
    ks = jax.random.split(key, 20)
    nrm = lambda k, shape, s: jax.random.normal(k, shape, jnp.float32) * s
    return {
        "x": nrm(ks[0], (BATCH, SEQ, D_MODEL), 1.0),
        "mem": nrm(ks[1], (BATCH, N_MEM, D_MODEL), 1.0),
        "ln_g": 1.0 + nrm(ks[2], (DEPTH, 3, D_MODEL), 0.01),
        "ln_b": nrm(ks[3], (DEPTH, 3, D_MODEL), 0.01),
        "ffn1_w_gu": nrm(ks[4], (DEPTH, D_MODEL, 2 * D_FF), D_MODEL ** -0.5),
        "ffn1_w_down": nrm(ks[5], (DEPTH, D_FF, D_MODEL), BETA * D_FF ** -0.5),
        "ffn2_w_gu": nrm(ks[6], (DEPTH, D_MODEL, 2 * D_FF), D_MODEL ** -0.5),
        "ffn2_w_down": nrm(ks[7], (DEPTH, D_FF, D_MODEL), BETA * D_FF ** -0.5),
        "mem_w_kv": nrm(ks[8], (DEPTH, D_MODEL, 2 * MEM_W), D_MODEL ** -0.5),
        "gla_w_in": nrm(ks[9], (N_GLA, D_MODEL, GLA_IN), D_MODEL ** -0.5),
        "gla_w_gate2": nrm(ks[10], (N_GLA, GLA_GATE_RANK, GLA_QK), GLA_GATE_RANK ** -0.5),
        "gla_b_gate": nrm(ks[11], (N_GLA, GLA_QK), 0.01),
        "gla_g_onorm": 1.0 + nrm(ks[12], (N_GLA, GLA_DV), 0.01),
        "gla_w_out": nrm(ks[13], (N_GLA, GLA_OUT_IN, D_MODEL), BETA * GLA_OUT_IN ** -0.5),
        "conv_w_in": nrm(ks[14], (N_CONV, D_MODEL, CONV_IN), D_MODEL ** -0.5),
        "conv_w_dw": nrm(ks[15], (N_CONV, CONV_WIDTH, CONV_CH), CONV_WIDTH ** -0.5),
        "conv_b_dw": nrm(ks[16], (N_CONV, CONV_CH), 0.01),
        "conv_ln_g": 1.0 + nrm(ks[17], (N_CONV, CONV_CH), 0.01),
        "conv_ln_b": nrm(ks[18], (N_CONV, CONV_CH), 0.01),
        "conv_w_out": nrm(ks[19], (N_CONV, CONV_OUT_IN, D_MODEL), BETA * CONV_OUT_IN ** -0.5),
    }


def reference(x, mem, ln_g, ln_b, ffn1_w_gu, ffn1_w_down, ffn2_w_gu, ffn2_w_down, mem_w_kv,
              gla_w_in, gla_w_gate2, gla_b_gate, gla_g_onorm, gla_w_out,
              conv_w_in, conv_w_dw, conv_b_dw, conv_ln_g, conv_ln_b, conv_w_out):
    for i in range(DEPTH):
        j = i // N_MIXERS
        x = layer_norm(ALPHA * x + 0.5 * swiglu(x, ffn1_w_gu[i], ffn1_w_down[i]), ln_g[i, 0], ln_b[i, 0])
        if i % N_MIXERS == 0:
            y, qm = gla_mix(x @ gla_w_in[j], gla_w_gate2[j], gla_b_gate[j], gla_g_onorm[j])
            w_out = gla_w_out[j]
        else:
            y, qm = conv_mix(x @ conv_w_in[j], conv_w_dw[j], conv_b_dw[j], conv_ln_g[j], conv_ln_b[j])
            w_out = conv_w_out[j]
        m = mem_attention(qm, mem, mem_w_kv[i])
        x = layer_norm(ALPHA * x + jnp.concatenate([y, m], axis=-1) @ w_out, ln_g[i, 1], ln_b[i, 1])
        x = layer_norm(ALPHA * x + 0.5 * swiglu(x, ffn2_w_gu[i], ffn2_w_down[i]), ln_g[i, 2], ln_b[i, 2])
    return x
```

```python
import functools

import jax
import jax.numpy as jnp
from jax import lax
from jax.experimental import pallas as pl
from jax.experimental.pallas import tpu as pltpu

F32 = jnp.float32
BF16 = jnp.bfloat16

D_MODEL = 1024
DEPTH = 4
N_MEM = 256
GLA_HEADS = 4
GLA_DK = 128
GLA_DV = 256
GLA_QK = GLA_HEADS * GLA_DK
GLA_V = GLA_HEADS * GLA_DV
GLA_GATE_RANK = 16
GLA_GATE_NORM = 16.0
GLA_CHUNK = 64
CONV_CH = D_MODEL
CONV_WIDTH = 31
MEM_HEADS = 4
MEM_DH = 128
MEM_W = MEM_HEADS * MEM_DH
D_FF = 2816
ALPHA = (2 * DEPTH) ** 0.25
LN_EPS = 1e-5
RMS_EPS = 1e-6

LANES = 128
FFN_TILE = 512
FFN_CHUNK = 256
FFN_NCHUNK = D_FF // FFN_CHUNK
MIX_TILE = 512
CONV_HALO = 32
CONV_ROWS = 64
VMEM_LIMIT = 56 * 1024 * 1024

GLA_GATE_PAD = LANES
GLA_IN_PAD = 2 * GLA_QK + 2 * GLA_V + MEM_W + GLA_GATE_PAD


def _resident(shape):
    nd = len(shape)
    return pl.BlockSpec(shape, lambda *_: (0,) * nd, pipeline_mode=pl.Buffered(1))


def _layer_norm(z, g, b):
    mu = jnp.mean(z, axis=-1, keepdims=True)
    zc = z - mu
    var = jnp.mean(zc * zc, axis=-1, keepdims=True)
    return zc * lax.rsqrt(var + LN_EPS) * g + b


def _silu(t):
    return t * jax.nn.sigmoid(t)


def _dot(a, b):
    return jnp.dot(a, b, preferred_element_type=F32)


def _dot_nt(a, b):
    return lax.dot_general(a, b, (((1,), (1,)), ((), ())), preferred_element_type=F32)


def _ffn_kernel(x_ref, wgu_ref, wd_ref, g_ref, b_ref, o_ref):
    x = x_ref[...]
    xb = x.astype(BF16)
    acc = jnp.zeros((FFN_TILE, D_MODEL), F32)
    for c in range(FFN_NCHUNK):
        h = _dot(xb, wgu_ref[c])
        gated = (_silu(h[:, :FFN_CHUNK]) * h[:, FFN_CHUNK:]).astype(BF16)
        acc = acc + _dot(gated, wd_ref[c])
    o_ref[...] = _layer_norm(ALPHA * x + 0.5 * acc, g_ref[...], b_ref[...])


def _ffn_ln(x2d, w_gu, w_down, g, b):
    t = x2d.shape[0]
    wgu = (w_gu.reshape(D_MODEL, 2, FFN_NCHUNK, FFN_CHUNK).transpose(2, 0, 1, 3)
           .reshape(FFN_NCHUNK, D_MODEL, 2 * FFN_CHUNK).astype(BF16))
    wd = w_down.reshape(FFN_NCHUNK, FFN_CHUNK, D_MODEL).astype(BF16)
    return pl.pallas_call(
        _ffn_kernel,
        grid=(t // FFN_TILE,),
        in_specs=[
            pl.BlockSpec((FFN_TILE, D_MODEL), lambda i: (i, 0)),
            _resident(wgu.shape),
            _resident(wd.shape),
            _resident((1, D_MODEL)),
            _resident((1, D_MODEL)),
        ],
        out_specs=pl.BlockSpec((FFN_TILE, D_MODEL), lambda i: (i, 0)),
        out_shape=jax.ShapeDtypeStruct((t, D_MODEL), F32),
        compiler_params=pltpu.CompilerParams(
            dimension_semantics=("arbitrary",), vmem_limit_bytes=VMEM_LIMIT),
        name="ffn_ln",
    )(x2d, wgu, wd, g.reshape(1, D_MODEL), b.reshape(1, D_MODEL))


def _kv_kernel(mem_ref, w_ref, o_ref):
    o_ref[0] = _dot(mem_ref[...], w_ref[0]).astype(BF16)


def _mem_kv(mem, mem_w_kv):
    b = mem.shape[0]
    mem2d = mem.reshape(b * N_MEM, D_MODEL).astype(BF16)
    kv = pl.pallas_call(
        _kv_kernel,
        grid=(DEPTH,),
        in_specs=[
            _resident(mem2d.shape),
            pl.BlockSpec((1, D_MODEL, 2 * MEM_W), lambda i: (i, 0, 0)),
        ],
        out_specs=pl.BlockSpec((1, b * N_MEM, 2 * MEM_W), lambda i: (i, 0, 0)),
        out_shape=jax.ShapeDtypeStruct((DEPTH, b * N_MEM, 2 * MEM_W), BF16),
        compiler_params=pltpu.CompilerParams(
            dimension_semantics=("arbitrary",), vmem_limit_bytes=VMEM_LIMIT),
        name="mem_kv",
    )(mem2d, mem_w_kv.astype(BF16))
    return kv.reshape(DEPTH, b, N_MEM, 2 * MEM_W)


def _mem_attention(qm, kv_ref):
    outs = []
    for hd in range(MEM_HEADS):
        lo = hd * MEM_DH
        q = qm[:, lo:lo + MEM_DH].astype(BF16)
        k = kv_ref[0, :, lo:lo + MEM_DH]
        v = kv_ref[0, :, MEM_W + lo:MEM_W + lo + MEM_DH]
        s = _dot_nt(q, k) * (MEM_DH ** -0.5)
        e = jnp.exp(s - jnp.max(s, axis=-1, keepdims=True))
        p = e * (1.0 / jnp.sum(e, axis=-1, keepdims=True))
        outs.append(_dot(p.astype(BF16), v).astype(BF16))
    return outs


def _gla_kernel(x_ref, win_ref, wg2_ref, bg_ref, gon_ref, kv_ref, wout_ref, lng_ref, lnb_ref,
                o_ref, state_ref, h_ref, oc_ref):
    @pl.when(pl.program_id(1) == 0)
    def _():
        state_ref[...] = jnp.zeros_like(state_ref)

    x = x_ref[0]
    h_ref[...] = _dot(x.astype(BF16), win_ref[...])
    c_q, c_k, c_v, c_r = 0, GLA_QK, 2 * GLA_QK, 2 * GLA_QK + GLA_V
    c_qm = c_r + GLA_V
    c_a = c_qm + MEM_W

    z = _dot(h_ref[:, c_a:c_a + GLA_GATE_PAD].astype(BF16), wg2_ref[...]) + bg_ref[...]
    lg = (jnp.minimum(z, 0.0) - jnp.log1p(jnp.exp(-jnp.abs(z)))) * (1.0 / GLA_GATE_NORM)

    row = lax.broadcasted_iota(jnp.int32, (GLA_CHUNK, GLA_CHUNK), 0)
    col = lax.broadcasted_iota(jnp.int32, (GLA_CHUNK, GLA_CHUNK), 1)
    causal = row >= col
    tril = causal.astype(BF16)

    for c in range(MIX_TILE // GLA_CHUNK):
        r0 = c * GLA_CHUNK
        rows = slice(r0, r0 + GLA_CHUNK)
        lgc = lg[rows, :]
        p0 = lgc.astype(BF16)
        e0 = lgc - p0.astype(F32)
        p1 = e0.astype(BF16)
        p2 = (e0 - p1.astype(F32)).astype(BF16)
        bcum = _dot(tril, p0) + _dot(tril, p1) + _dot(tril, p2)
        for hd in range(GLA_HEADS):
            kl = hd * GLA_DK
            vl = hd * GLA_DV
            bc = bcum[:, kl:kl + GLA_DK]
            qc = h_ref[rows, c_q + kl:c_q + kl + GLA_DK]
            kc = h_ref[rows, c_k + kl:c_k + kl + GLA_DK]
            vc = h_ref[rows, c_v + vl:c_v + vl + GLA_DV].astype(BF16)
            qe = ((qc * (GLA_DK ** -0.5)) * jnp.exp(bc)).astype(BF16)
            ke = (kc * jnp.exp(-bc)).astype(BF16)
            att = jnp.where(causal, _dot_nt(qe, ke), 0.0)
            state = state_ref[hd]
            o = _dot(att.astype(BF16), vc) + _dot(qe, state.astype(BF16))
            oc_ref[rows, vl:vl + GLA_DV] = o
            bc_t = bc.T
            bl_t = bc_t[:, GLA_CHUNK - 1:GLA_CHUNK]
            kd_t = (kc.T * jnp.exp(bl_t - bc_t)).astype(BF16)
            state_ref[hd] = state * jnp.exp(bl_t) + _dot(kd_t, vc)

    gon = gon_ref[...]
    parts = []
    for hd in range(GLA_HEADS):
        vl = hd * GLA_DV
        o = oc_ref[:, vl:vl + GLA_DV]
        o = o * lax.rsqrt(jnp.mean(o * o, axis=-1, keepdims=True) + RMS_EPS) * gon
        r = h_ref[:, c_r + vl:c_r + vl + GLA_DV]
        parts.append((o * _silu(r)).astype(BF16))
    parts += _mem_attention(h_ref[:, c_qm:c_qm + MEM_W], kv_ref)
    y = _dot(jnp.concatenate(parts, axis=-1), wout_ref[...])
    o_ref[0] = _layer_norm(ALPHA * x + y, lng_ref[...], lnb_ref[...])


def _gla_layer(x, kv, w_in, w_gate2, b_gate, g_onorm, w_out, ln_g, ln_b):
    b, s, _ = x.shape
    n_main = 2 * GLA_QK + 2 * GLA_V
    win = jnp.concatenate(
        [w_in[:, :n_main], w_in[:, n_main + GLA_GATE_RANK:],
         jnp.pad(w_in[:, n_main:n_main + GLA_GATE_RANK], ((0, 0), (0, GLA_GATE_PAD - GLA_GATE_RANK)))],
        axis=1).astype(BF16)
    wg2 = jnp.pad(w_gate2, ((0, GLA_GATE_PAD - GLA_GATE_RANK), (0, 0))).astype(BF16)
    return pl.pallas_call(
        _gla_kernel,
        grid=(b, s // MIX_TILE),
        in_specs=[
            pl.BlockSpec((1, MIX_TILE, D_MODEL), lambda i, j: (i, j, 0)),
            _resident(win.shape),
            _resident(wg2.shape),
            _resident((1, GLA_QK)),
            _resident((1, GLA_DV)),
            pl.BlockSpec((1, N_MEM, 2 * MEM_W), lambda i, j: (i, 0, 0)),
            _resident((GLA_V + MEM_W, D_MODEL)),
            _resident((1, D_MODEL)),
            _resident((1, D_MODEL)),
        ],
        out_specs=pl.BlockSpec((1, MIX_TILE, D_MODEL), lambda i, j: (i, j, 0)),
        out_shape=jax.ShapeDtypeStruct((b, s, D_MODEL), F32),
        scratch_shapes=[
            pltpu.VMEM((GLA_HEADS, GLA_DK, GLA_DV), F32),
            pltpu.VMEM((MIX_TILE, GLA_IN_PAD), F32),
            pltpu.VMEM((MIX_TILE, GLA_V), F32),
        ],
        compiler_params=pltpu.CompilerParams(
            dimension_semantics=("arbitrary", "arbitrary"), vmem_limit_bytes=VMEM_LIMIT),
        name="gla_mixer",
    )(x, win, wg2, b_gate.reshape(1, GLA_QK), g_onorm.reshape(1, GLA_DV), kv,
      w_out.astype(BF16), ln_g.reshape(1, D_MODEL), ln_b.reshape(1, D_MODEL))


def _conv_kernel(x_ref, win_ref, wdw_ref, bdw_ref, cg_ref, cb_ref, kv_ref, wout_ref, lng_ref, lnb_ref,
                 o_ref, ext_ref, h_ref, cv_ref):
    @pl.when(pl.program_id(1) == 0)
    def _():
        ext_ref[0:CONV_HALO, :] = jnp.zeros((CONV_HALO, CONV_CH), F32)

    x = x_ref[0]
    h_ref[...] = _dot(x.astype(BF16), win_ref[...])
    ext_ref[CONV_HALO:, :] = h_ref[:, :CONV_CH] * jax.nn.sigmoid(h_ref[:, CONV_CH:2 * CONV_CH])

    off = CONV_HALO - (CONV_WIDTH - 1)
    for rb in range(MIX_TILE // CONV_ROWS):
        for cb in range(CONV_CH // LANES):
            cols = slice(cb * LANES, (cb + 1) * LANES)
            acc = jnp.broadcast_to(bdw_ref[:, cols], (CONV_ROWS, LANES))
            for j in range(CONV_WIDTH):
                r0 = rb * CONV_ROWS + off + j
                acc = acc + wdw_ref[j:j + 1, cols] * ext_ref[r0:r0 + CONV_ROWS, cols]
            cv_ref[rb * CONV_ROWS:(rb + 1) * CONV_ROWS, cols] = acc
    ext_ref[0:CONV_HALO, :] = ext_ref[MIX_TILE:MIX_TILE + CONV_HALO, :]

    y = _silu(_layer_norm(cv_ref[...], cg_ref[...], cb_ref[...])).astype(BF16)
    parts = [y] + _mem_attention(h_ref[:, 2 * CONV_CH:], kv_ref)
    out = _dot(jnp.concatenate(parts, axis=-1), wout_ref[...])
    o_ref[0] = _layer_norm(ALPHA * x + out, lng_ref[...], lnb_ref[...])


def _conv_layer(x, kv, w_in, w_dw, b_dw, c_g, c_b, w_out, ln_g, ln_b):
    b, s, _ = x.shape
    conv_in = 2 * CONV_CH + MEM_W
    return pl.pallas_call(
        _conv_kernel,
        grid=(b, s // MIX_TILE),
        in_specs=[
            pl.BlockSpec((1, MIX_TILE, D_MODEL), lambda i, j: (i, j, 0)),
            _resident((D_MODEL, conv_in)),
            _resident((CONV_WIDTH, CONV_CH)),
            _resident((1, CONV_CH)),
            _resident((1, CONV_CH)),
            _resident((1, CONV_CH)),
            pl.BlockSpec((1, N_MEM, 2 * MEM_W), lambda i, j: (i, 0, 0)),
            _resident((CONV_CH + MEM_W, D_MODEL)),
            _resident((1, D_MODEL)),
            _resident((1, D_MODEL)),
        ],
        out_specs=pl.BlockSpec((1, MIX_TILE, D_MODEL), lambda i, j: (i, j, 0)),
        out_shape=jax.ShapeDtypeStruct((b, s, D_MODEL), F32),
        scratch_shapes=[
            pltpu.VMEM((CONV_HALO + MIX_TILE, CONV_CH), F32),
            pltpu.VMEM((MIX_TILE, conv_in), F32),
            pltpu.VMEM((MIX_TILE, CONV_CH), F32),
        ],
        compiler_params=pltpu.CompilerParams(
            dimension_semantics=("arbitrary", "arbitrary"), vmem_limit_bytes=VMEM_LIMIT),
        name="conv_mixer",
    )(x, w_in.astype(BF16), w_dw, b_dw.reshape(1, CONV_CH), c_g.reshape(1, CONV_CH),
      c_b.reshape(1, CONV_CH), kv, w_out.astype(BF16), ln_g.reshape(1, D_MODEL), ln_b.reshape(1, D_MODEL))


def kernel(x, mem, ln_g, ln_b, ffn1_w_gu, ffn1_w_down, ffn2_w_gu, ffn2_w_down, mem_w_kv, gla_w_in, gla_w_gate2, gla_b_gate, gla_g_onorm, gla_w_out, conv_w_in, conv_w_dw, conv_b_dw, conv_ln_g, conv_ln_b, conv_w_out):
    b, s, d = x.shape
    kv = _mem_kv(mem, mem_w_kv)
    for i in range(DEPTH):
        j = i // 2
        x = _ffn_ln(x.reshape(b * s, d), ffn1_w_gu[i], ffn1_w_down[i], ln_g[i, 0], ln_b[i, 0]).reshape(b, s, d)
        if i % 2 == 0:
            x = _gla_layer(x, kv[i], gla_w_in[j], gla_w_gate2[j], gla_b_gate[j], gla_g_onorm[j],
                           gla_w_out[j], ln_g[i, 1], ln_b[i, 1])
        else:
            x = _conv_layer(x, kv[i], conv_w_in[j], conv_w_dw[j], conv_b_dw[j], conv_ln_g[j], conv_ln_b[j],
                            conv_w_out[j], ln_g[i, 1], ln_b[i, 1])
        x = _ffn_ln(x.reshape(b * s, d), ffn2_w_gu[i], ffn2_w_down[i], ln_g[i, 2], ln_b[i, 2]).reshape(b, s, d)
    return x
```

```python
import jax
import jax.numpy as jnp
from jax import lax
from jax.experimental import pallas as pl
from jax.experimental.pallas import tpu as pltpu

F32 = jnp.float32
BF16 = jnp.bfloat16

D_MODEL = 1024
DEPTH = 4
N_MEM = 256
GLA_HEADS = 4
GLA_DK = 128
GLA_DV = 256
GLA_QK = GLA_HEADS * GLA_DK
GLA_V = GLA_HEADS * GLA_DV
GLA_GATE_RANK = 16
GLA_GATE_NORM = 16.0
GLA_CHUNK = 64
CONV_CH = D_MODEL
CONV_WIDTH = 31
MEM_HEADS = 4
MEM_DH = 128
MEM_W = MEM_HEADS * MEM_DH
D_FF = 2816
ALPHA = (2 * DEPTH) ** 0.25
LN_EPS = 1e-5
RMS_EPS = 1e-6

LANES = 128
FFN_TILE = 512
FFN_CHUNK = 256
FFN_NCHUNK = D_FF // FFN_CHUNK
MIX_TILE = 512
VMEM_LIMIT = 56 * 1024 * 1024

GLA_GATE_PAD = LANES
GLA_IN_PAD = 2 * GLA_QK + 2 * GLA_V + MEM_W + GLA_GATE_PAD


def _resident(shape):
    nd = len(shape)
    return pl.BlockSpec(shape, lambda *_: (0,) * nd, pipeline_mode=pl.Buffered(1))


def _layer_norm(z, g, b):
    mu = jnp.mean(z, axis=-1, keepdims=True)
    zc = z - mu
    var = jnp.mean(zc * zc, axis=-1, keepdims=True)
    return zc * lax.rsqrt(var + LN_EPS) * g + b


def _silu(t):
    return t * jax.nn.sigmoid(t)


def _dot(a, b):
    return jnp.dot(a, b, preferred_element_type=F32)


def _dot_nt(a, b):
    return lax.dot_general(a, b, (((1,), (1,)), ((), ())), preferred_element_type=F32)


def _ffn_kernel(x_ref, wgu_ref, wd_ref, g_ref, b_ref, o_ref):
    x = x_ref[...]
    xb = x.astype(BF16)
    acc = jnp.zeros((FFN_TILE, D_MODEL), F32)
    for c in range(FFN_NCHUNK):
        lo = c * FFN_CHUNK
        gate = _dot(xb, wgu_ref[:, lo:lo + FFN_CHUNK])
        up = _dot(xb, wgu_ref[:, D_FF + lo:D_FF + lo + FFN_CHUNK])
        acc = acc + _dot((_silu(gate) * up).astype(BF16), wd_ref[lo:lo + FFN_CHUNK, :])
    o_ref[...] = _layer_norm(ALPHA * x + 0.5 * acc, g_ref[...], b_ref[...])


def _ffn_ln(x2d, w_gu, w_down, g, b):
    t = x2d.shape[0]
    wgu = w_gu.astype(BF16)
    wd = w_down.astype(BF16)
    return pl.pallas_call(
        _ffn_kernel,
        grid=(t // FFN_TILE,),
        in_specs=[
            pl.BlockSpec((FFN_TILE, D_MODEL), lambda i: (i, 0)),
            _resident(wgu.shape),
            _resident(wd.shape),
            _resident((1, D_MODEL)),
            _resident((1, D_MODEL)),
        ],
        out_specs=pl.BlockSpec((FFN_TILE, D_MODEL), lambda i: (i, 0)),
        out_shape=jax.ShapeDtypeStruct((t, D_MODEL), F32),
        compiler_params=pltpu.CompilerParams(
            dimension_semantics=("arbitrary",), vmem_limit_bytes=VMEM_LIMIT),
        name="ffn_ln",
    )(x2d, wgu, wd, g.reshape(1, D_MODEL), b.reshape(1, D_MODEL))


def _kv_kernel(mem_ref, w_ref, o_ref):
    o_ref[0] = _dot(mem_ref[...], w_ref[0]).astype(BF16)


def _mem_kv(mem, mem_w_kv):
    b = mem.shape[0]
    mem2d = mem.reshape(b * N_MEM, D_MODEL).astype(BF16)
    kv = pl.pallas_call(
        _kv_kernel,
        grid=(DEPTH,),
        in_specs=[
            _resident(mem2d.shape),
            pl.BlockSpec((1, D_MODEL, 2 * MEM_W), lambda i: (i, 0, 0)),
        ],
        out_specs=pl.BlockSpec((1, b * N_MEM, 2 * MEM_W), lambda i: (i, 0, 0)),
        out_shape=jax.ShapeDtypeStruct((DEPTH, b * N_MEM, 2 * MEM_W), BF16),
        compiler_params=pltpu.CompilerParams(
            dimension_semantics=("arbitrary",), vmem_limit_bytes=VMEM_LIMIT),
        name="mem_kv",
    )(mem2d, mem_w_kv.astype(BF16))
    return kv.reshape(DEPTH, b, N_MEM, 2 * MEM_W)


def _mem_attention(qm, kv_ref):
    outs = []
    for hd in range(MEM_HEADS):
        lo = hd * MEM_DH
        q = qm[:, lo:lo + MEM_DH].astype(BF16)
        k = kv_ref[0, :, lo:lo + MEM_DH]
        v = kv_ref[0, :, MEM_W + lo:MEM_W + lo + MEM_DH]
        s = _dot_nt(q, k) * (MEM_DH ** -0.5)
        e = jnp.exp(s - jnp.max(s, axis=-1, keepdims=True))
        p = e * (1.0 / jnp.sum(e, axis=-1, keepdims=True))
        outs.append(_dot(p.astype(BF16), v).astype(BF16))
    return outs


def _gla_kernel(x_ref, win_ref, wg2_ref, bg_ref, gon_ref, kv_ref, wout_ref, lng_ref, lnb_ref,
                o_ref, state_ref, h_ref, oc_ref):
    @pl.when(pl.program_id(1) == 0)
    def _():
        state_ref[...] = jnp.zeros_like(state_ref)

    x = x_ref[0]
    h_ref[...] = _dot(x.astype(BF16), win_ref[...])
    c_q, c_k, c_v, c_r = 0, GLA_QK, 2 * GLA_QK, 2 * GLA_QK + GLA_V
    c_qm = c_r + GLA_V
    c_a = c_qm + MEM_W

    z = _dot(h_ref[:, c_a:c_a + GLA_GATE_PAD].astype(BF16), wg2_ref[...]) + bg_ref[...]
    lg = (jnp.minimum(z, 0.0) - jnp.log1p(jnp.exp(-jnp.abs(z)))) * (1.0 / GLA_GATE_NORM)

    row = lax.broadcasted_iota(jnp.int32, (GLA_CHUNK, GLA_CHUNK), 0)
    col = lax.broadcasted_iota(jnp.int32, (GLA_CHUNK, GLA_CHUNK), 1)
    causal = row >= col
    tril = causal.astype(BF16)

    for c in range(MIX_TILE // GLA_CHUNK):
        r0 = c * GLA_CHUNK
        rows = slice(r0, r0 + GLA_CHUNK)
        lgc = lg[rows, :]
        p0 = lgc.astype(BF16)
        e0 = lgc - p0.astype(F32)
        p1 = e0.astype(BF16)
        p2 = (e0 - p1.astype(F32)).astype(BF16)
        bcum = _dot(tril, p0) + _dot(tril, p1) + _dot(tril, p2)
        for hd in range(GLA_HEADS):
            kl = hd * GLA_DK
            vl = hd * GLA_DV
            bc = bcum[:, kl:kl + GLA_DK]
            qc = h_ref[rows, c_q + kl:c_q + kl + GLA_DK]
            kc = h_ref[rows, c_k + kl:c_k + kl + GLA_DK]
            vc = h_ref[rows, c_v + vl:c_v + vl + GLA_DV].astype(BF16)
            qe = ((qc * (GLA_DK ** -0.5)) * jnp.exp(bc)).astype(BF16)
            ke = (kc * jnp.exp(-bc)).astype(BF16)
            att = jnp.where(causal, _dot_nt(qe, ke), 0.0)
            state = state_ref[hd]
            o = _dot(att.astype(BF16), vc) + _dot(qe, state.astype(BF16))
            oc_ref[rows, vl:vl + GLA_DV] = o
            bc_t = bc.T
            bl_t = bc_t[:, GLA_CHUNK - 1:GLA_CHUNK]
            kd_t = (kc.T * jnp.exp(bl_t - bc_t)).astype(BF16)
            state_ref[hd] = state * jnp.exp(bl_t) + _dot(kd_t, vc)

    gon = gon_ref[...]
    parts = []
    for hd in range(GLA_HEADS):
        vl = hd * GLA_DV
        o = oc_ref[:, vl:vl + GLA_DV]
        o = o * lax.rsqrt(jnp.mean(o * o, axis=-1, keepdims=True) + RMS_EPS) * gon
        r = h_ref[:, c_r + vl:c_r + vl + GLA_DV]
        parts.append((o * _silu(r)).astype(BF16))
    parts += _mem_attention(h_ref[:, c_qm:c_qm + MEM_W], kv_ref)
    y = _dot(jnp.concatenate(parts, axis=-1), wout_ref[...])
    o_ref[0] = _layer_norm(ALPHA * x + y, lng_ref[...], lnb_ref[...])


def _gla_layer(x, kv, w_in, w_gate2, b_gate, g_onorm, w_out, ln_g, ln_b):
    b, s, _ = x.shape
    n_main = 2 * GLA_QK + 2 * GLA_V
    win = jnp.concatenate(
        [w_in[:, :n_main], w_in[:, n_main + GLA_GATE_RANK:],
         jnp.pad(w_in[:, n_main:n_main + GLA_GATE_RANK], ((0, 0), (0, GLA_GATE_PAD - GLA_GATE_RANK)))],
        axis=1).astype(BF16)
    wg2 = jnp.pad(w_gate2, ((0, GLA_GATE_PAD - GLA_GATE_RANK), (0, 0))).astype(BF16)
    return pl.pallas_call(
        _gla_kernel,
        grid=(b, s // MIX_TILE),
        in_specs=[
            pl.BlockSpec((1, MIX_TILE, D_MODEL), lambda i, j: (i, j, 0)),
            _resident(win.shape),
            _resident(wg2.shape),
            _resident((1, GLA_QK)),
            _resident((1, GLA_DV)),
            pl.BlockSpec((1, N_MEM, 2 * MEM_W), lambda i, j: (i, 0, 0)),
            _resident((GLA_V + MEM_W, D_MODEL)),
            _resident((1, D_MODEL)),
            _resident((1, D_MODEL)),
        ],
        out_specs=pl.BlockSpec((1, MIX_TILE, D_MODEL), lambda i, j: (i, j, 0)),
        out_shape=jax.ShapeDtypeStruct((b, s, D_MODEL), F32),
        scratch_shapes=[
            pltpu.VMEM((GLA_HEADS, GLA_DK, GLA_DV), F32),
            pltpu.VMEM((MIX_TILE, GLA_IN_PAD), F32),
            pltpu.VMEM((MIX_TILE, GLA_V), F32),
        ],
        compiler_params=pltpu.CompilerParams(
            dimension_semantics=("arbitrary", "arbitrary"), vmem_limit_bytes=VMEM_LIMIT),
        name="gla_mixer",
    )(x, win, wg2, b_gate.reshape(1, GLA_QK), g_onorm.reshape(1, GLA_DV), kv,
      w_out.astype(BF16), ln_g.reshape(1, D_MODEL), ln_b.reshape(1, D_MODEL))


CONV_SEG = MIX_TILE // 8
CONV_HALO_V = 32
CONV_HALO_ROWS = CONV_HALO_V * 8
CONV_GROUP = 256
CONV_NGROUP = CONV_CH // CONV_GROUP
CONV_NSLAB = CONV_CH // LANES
MIX_SUB = 256


def _conv_kernel(x_ref, win_ref, w16_ref, bdw_ref, cg_ref, cb_ref, kv_ref, wout_ref, lng_ref, lnb_ref,
                 o_ref, ext_ref, exta_ref, extb_ref, cvp_ref, cv_ref):
    @pl.when(pl.program_id(1) == 0)
    def _():
        ext_ref[:, 0:CONV_HALO_ROWS, :] = jnp.zeros((CONV_NSLAB, CONV_HALO_ROWS, LANES), F32)

    x = x_ref[0]
    xb = x.astype(BF16)
    off = CONV_HALO_V - (CONV_WIDTH - 1)
    ext_rows = CONV_HALO_ROWS + MIX_TILE
    slabs_per_group = CONV_GROUP // LANES

    def project(g):
        lo = g * CONV_GROUP
        value = _dot(xb, win_ref[:, lo:lo + CONV_GROUP])
        gate = _dot(xb, win_ref[:, CONV_CH + lo:CONV_CH + lo + CONV_GROUP])
        u = value * jax.nn.sigmoid(gate)
        for half in range(slabs_per_group):
            sl = g * slabs_per_group + half
            us = u[:, half * LANES:(half + 1) * LANES]
            for s in range(8):
                seg = us[s * CONV_SEG:(s + 1) * CONV_SEG, :]
                ext_ref[sl, pl.ds(CONV_HALO_ROWS + s, CONV_SEG, stride=8), :] = seg
                if s < 7:
                    ext_ref[sl, pl.ds(s + 1, CONV_HALO_V, stride=8), :] = seg[CONV_SEG - CONV_HALO_V:, :]
            exta_ref[sl] = ext_ref[sl].astype(BF16)
            extb_ref[sl, 0:ext_rows - 16, :] = ext_ref[sl, 8:ext_rows - 8, :].astype(BF16)
            ext_ref[sl, pl.ds(0, CONV_HALO_V, stride=8), :] = us[MIX_TILE - CONV_HALO_V:, :]

    def convolve(g):
        for half in range(slabs_per_group):
            sl = g * slabs_per_group + half
            cols = slice(sl * LANES, (sl + 1) * LANES)
            bias = bdw_ref[:, cols]
            for k in range(MIX_TILE // 16):
                acc = None
                for j in range(CONV_WIDTH):
                    m = 2 * k + j + off
                    src = exta_ref if m % 2 == 0 else extb_ref
                    r0 = (m // 2) * 16
                    term = src[sl, r0:r0 + 16, :].astype(F32) * w16_ref[j, sl].astype(F32)
                    acc = term if acc is None else acc + term
                cvp_ref[sl, k * 16:(k + 1) * 16, :] = acc + bias
            for s in range(8):
                cv_ref[s * CONV_SEG:(s + 1) * CONV_SEG, cols] = cvp_ref[sl, pl.ds(s, CONV_SEG, stride=8), :]

    project(0)
    for g in range(1, CONV_NGROUP):
        project(g)
        convolve(g - 1)
    mem = _mem_attention(_dot(xb, win_ref[:, 2 * CONV_CH:]), kv_ref)
    convolve(CONV_NGROUP - 1)

    for sb in range(MIX_TILE // MIX_SUB):
        rows = slice(sb * MIX_SUB, (sb + 1) * MIX_SUB)
        y = _silu(_layer_norm(cv_ref[rows, :], cg_ref[...], cb_ref[...])).astype(BF16)
        parts = [y] + [m[rows, :] for m in mem]
        out = _dot(jnp.concatenate(parts, axis=-1), wout_ref[...])
        o_ref[0, rows, :] = _layer_norm(ALPHA * x[rows, :] + out, lng_ref[...], lnb_ref[...])


def _conv_layer(x, kv, w_in, w_dw, b_dw, c_g, c_b, w_out, ln_g, ln_b):
    b, s, _ = x.shape
    win = w_in.astype(BF16)
    w16 = jnp.broadcast_to(w_dw.astype(BF16).reshape(CONV_WIDTH, CONV_NSLAB, 1, LANES),
                           (CONV_WIDTH, CONV_NSLAB, 16, LANES))
    return pl.pallas_call(
        _conv_kernel,
        grid=(b, s // MIX_TILE),
        in_specs=[
            pl.BlockSpec((1, MIX_TILE, D_MODEL), lambda i, j: (i, j, 0)),
            _resident(win.shape),
            _resident((CONV_WIDTH, CONV_NSLAB, 16, LANES)),
            _resident((1, CONV_CH)),
            _resident((1, CONV_CH)),
            _resident((1, CONV_CH)),
            pl.BlockSpec((1, N_MEM, 2 * MEM_W), lambda i, j: (i, 0, 0)),
            _resident((CONV_CH + MEM_W, D_MODEL)),
            _resident((1, D_MODEL)),
            _resident((1, D_MODEL)),
        ],
        out_specs=pl.BlockSpec((1, MIX_TILE, D_MODEL), lambda i, j: (i, j, 0)),
        out_shape=jax.ShapeDtypeStruct((b, s, D_MODEL), F32),
        scratch_shapes=[
            pltpu.VMEM((CONV_NSLAB, CONV_HALO_ROWS + MIX_TILE, LANES), F32),
            pltpu.VMEM((CONV_NSLAB, CONV_HALO_ROWS + MIX_TILE, LANES), BF16),
            pltpu.VMEM((CONV_NSLAB, CONV_HALO_ROWS + MIX_TILE, LANES), BF16),
            pltpu.VMEM((CONV_NSLAB, MIX_TILE, LANES), F32),
            pltpu.VMEM((MIX_TILE, CONV_CH), F32),
        ],
        compiler_params=pltpu.CompilerParams(
            dimension_semantics=("arbitrary", "arbitrary"), vmem_limit_bytes=VMEM_LIMIT),
        name="conv_mixer",
    )(x, win, w16, b_dw.reshape(1, CONV_CH), c_g.reshape(1, CONV_CH),
      c_b.reshape(1, CONV_CH), kv, w_out.astype(BF16), ln_g.reshape(1, D_MODEL), ln_b.reshape(1, D_MODEL))


def kernel(x, mem, ln_g, ln_b, ffn1_w_gu, ffn1_w_down, ffn2_w_gu, ffn2_w_down, mem_w_kv, gla_w_in, gla_w_gate2, gla_b_gate, gla_g_onorm, gla_w_out, conv_w_in, conv_w_dw, conv_b_dw, conv_ln_g, conv_ln_b, conv_w_out):
    b, s, d = x.shape
    kv = _mem_kv(mem, mem_w_kv)
    for i in range(DEPTH):
        j = i // 2
        x = _ffn_ln(x.reshape(b * s, d), ffn1_w_gu[i], ffn1_w_down[i], ln_g[i, 0], ln_b[i, 0]).reshape(b, s, d)
        if i % 2 == 0:
            x = _gla_layer(x, kv[i], gla_w_in[j], gla_w_gate2[j], gla_b_gate[j], gla_g_onorm[j],
                           gla_w_out[j], ln_g[i, 1], ln_b[i, 1])
        else:
            x = _conv_layer(x, kv[i], conv_w_in[j], conv_w_dw[j], conv_b_dw[j], conv_ln_g[j], conv_ln_b[j],
                            conv_w_out[j], ln_g[i, 1], ln_b[i, 1])
        x = _ffn_ln(x.reshape(b * s, d), ffn2_w_gu[i], ffn2_w_down[i], ln_g[i, 2], ln_b[i, 2]).reshape(b, s, d)
    return x
```

```python
import jax
import jax.numpy as jnp
from jax import lax
from jax.experimental import pallas as pl
from jax.experimental.pallas import tpu as pltpu

F32 = jnp.float32
BF16 = jnp.bfloat16

D_MODEL = 1024
DEPTH = 4
N_MEM = 256
GLA_HEADS = 4
GLA_DK = 128
GLA_DV = 256
GLA_QK = GLA_HEADS * GLA_DK
GLA_V = GLA_HEADS * GLA_DV
GLA_GATE_RANK = 16
GLA_GATE_NORM = 16.0
GLA_CHUNK = 64
CONV_CH = D_MODEL
CONV_WIDTH = 31
MEM_HEADS = 4
MEM_DH = 128
MEM_W = MEM_HEADS * MEM_DH
D_FF = 2816
ALPHA = (2 * DEPTH) ** 0.25
LN_EPS = 1e-5
RMS_EPS = 1e-6

LANES = 128
FFN_TILE = 512
FFN_CHUNK = 256
FFN_NCHUNK = D_FF // FFN_CHUNK
MIX_TILE = 512
MIX_SUB = 256
VMEM_LIMIT = 56 * 1024 * 1024

GLA_GATE_PAD = LANES
GLA_IN_PAD = 2 * GLA_QK + 2 * GLA_V + MEM_W + GLA_GATE_PAD


def _resident(shape):
    nd = len(shape)
    return pl.BlockSpec(shape, lambda *_: (0,) * nd, pipeline_mode=pl.Buffered(1))


def _layer_slab(shape, layer):
    nd = len(shape)
    return pl.BlockSpec((1,) + tuple(shape[1:]), lambda *_: (layer,) + (0,) * (nd - 1),
                        pipeline_mode=pl.Buffered(1))


def _layer_norm(z, g, b):
    mu = jnp.mean(z, axis=-1, keepdims=True)
    zc = z - mu
    var = jnp.mean(zc * zc, axis=-1, keepdims=True)
    return zc * lax.rsqrt(var + LN_EPS) * g + b


def _silu(t):
    return t * jax.nn.sigmoid(t)


def _dot(a, b):
    return jnp.dot(a, b, preferred_element_type=F32)


def _dot_nt(a, b):
    return lax.dot_general(a, b, (((1,), (1,)), ((), ())), preferred_element_type=F32)


def _ffn_kernel(x_ref, wgu_ref, wd_ref, g_ref, b_ref, o_ref):
    x = x_ref[...]
    xb = x.astype(BF16)
    acc = jnp.zeros((FFN_TILE, D_MODEL), F32)
    for c in range(FFN_NCHUNK):
        lo = c * FFN_CHUNK
        gate = _dot(xb, wgu_ref[0, :, lo:lo + FFN_CHUNK])
        up = _dot(xb, wgu_ref[0, :, D_FF + lo:D_FF + lo + FFN_CHUNK])
        acc = acc + _dot((_silu(gate) * up).astype(BF16), wd_ref[0, lo:lo + FFN_CHUNK, :])
    o_ref[...] = _layer_norm(ALPHA * x + 0.5 * acc, g_ref[...], b_ref[...])


def _ffn_ln(x2d, wgu, wd, layer, g, b):
    t = x2d.shape[0]
    return pl.pallas_call(
        _ffn_kernel,
        grid=(t // FFN_TILE,),
        in_specs=[
            pl.BlockSpec((FFN_TILE, D_MODEL), lambda i: (i, 0)),
            _layer_slab(wgu.shape, layer),
            _layer_slab(wd.shape, layer),
            _resident((1, D_MODEL)),
            _resident((1, D_MODEL)),
        ],
        out_specs=pl.BlockSpec((FFN_TILE, D_MODEL), lambda i: (i, 0)),
        out_shape=jax.ShapeDtypeStruct((t, D_MODEL), F32),
        compiler_params=pltpu.CompilerParams(
            dimension_semantics=("arbitrary",), vmem_limit_bytes=VMEM_LIMIT),
        name="ffn_ln",
    )(x2d, wgu, wd, g.reshape(1, D_MODEL), b.reshape(1, D_MODEL))


def _chunk_cumsum(t):
    pos = lax.broadcasted_iota(jnp.int32, t.shape, 0) % GLA_CHUNK
    step = 1
    while step < GLA_CHUNK:
        t = t + jnp.where(pos >= step, pltpu.roll(t, step, 0), 0.0)
        step *= 2
    return t


def _kv_kernel(mem_ref, w_ref, o_ref):
    o_ref[0] = _dot(mem_ref[...], w_ref[0]).astype(BF16)


def _mem_kv(mem, mem_w_kv):
    b = mem.shape[0]
    mem2d = mem.reshape(b * N_MEM, D_MODEL).astype(BF16)
    kv = pl.pallas_call(
        _kv_kernel,
        grid=(DEPTH,),
        in_specs=[
            _resident(mem2d.shape),
            pl.BlockSpec((1, D_MODEL, 2 * MEM_W), lambda i: (i, 0, 0)),
        ],
        out_specs=pl.BlockSpec((1, b * N_MEM, 2 * MEM_W), lambda i: (i, 0, 0)),
        out_shape=jax.ShapeDtypeStruct((DEPTH, b * N_MEM, 2 * MEM_W), BF16),
        compiler_params=pltpu.CompilerParams(
            dimension_semantics=("arbitrary",), vmem_limit_bytes=VMEM_LIMIT),
        name="mem_kv",
    )(mem2d, mem_w_kv.astype(BF16))
    return kv.reshape(DEPTH, b, N_MEM, 2 * MEM_W)


def _mem_attention(qm, kv_ref):
    outs = []
    for hd in range(MEM_HEADS):
        lo = hd * MEM_DH
        q = qm[:, lo:lo + MEM_DH].astype(BF16)
        k = kv_ref[0, :, lo:lo + MEM_DH]
        v = kv_ref[0, :, MEM_W + lo:MEM_W + lo + MEM_DH]
        s = _dot_nt(q, k) * (MEM_DH ** -0.5)
        e = jnp.exp(s - jnp.max(s, axis=-1, keepdims=True))
        p = e * (1.0 / jnp.sum(e, axis=-1, keepdims=True))
        outs.append(_dot(p.astype(BF16), v).astype(BF16))
    return outs


def _gla_kernel(x_ref, win_ref, wg2_ref, bg_ref, gon_ref, kv_ref, wout_ref, lng_ref, lnb_ref,
                o_ref, state_ref, h_ref, oc_ref):
    @pl.when(pl.program_id(1) == 0)
    def _():
        state_ref[...] = jnp.zeros_like(state_ref)

    x = x_ref[0]
    h_ref[...] = _dot(x.astype(BF16), win_ref[...])
    c_q, c_k, c_v, c_r = 0, GLA_QK, 2 * GLA_QK, 2 * GLA_QK + GLA_V
    c_qm = c_r + GLA_V
    c_a = c_qm + MEM_W

    z = _dot(h_ref[:, c_a:c_a + GLA_GATE_PAD].astype(BF16), wg2_ref[...]) + bg_ref[...]
    lg = (jnp.minimum(z, 0.0) - jnp.log1p(jnp.exp(-jnp.abs(z)))) * (1.0 / GLA_GATE_NORM)
    bc = _chunk_cumsum(lg)
    n_chunk = MIX_TILE // GLA_CHUNK
    last = [bc[(c + 1) * GLA_CHUNK - 1:(c + 1) * GLA_CHUNK, :] for c in range(n_chunk)]
    bl = jnp.concatenate([jnp.broadcast_to(t, (GLA_CHUNK, GLA_QK)) for t in last], axis=0)
    q = h_ref[:, c_q:c_q + GLA_QK]
    k = h_ref[:, c_k:c_k + GLA_QK]
    qe = ((q * (GLA_DK ** -0.5)) * jnp.exp(bc)).astype(BF16)
    ke = (k * jnp.exp(-bc)).astype(BF16)
    kd = k * jnp.exp(bl - bc)
    dec_t = jnp.exp(jnp.concatenate(last, axis=0)).T

    row = lax.broadcasted_iota(jnp.int32, (GLA_CHUNK, GLA_CHUNK), 0)
    col = lax.broadcasted_iota(jnp.int32, (GLA_CHUNK, GLA_CHUNK), 1)
    causal = row >= col
    units = [(c, hd) for c in range(n_chunk) for hd in range(GLA_HEADS)]

    def rows_of(c):
        return slice(c * GLA_CHUNK, (c + 1) * GLA_CHUNK)

    def dk_of(hd):
        return slice(hd * GLA_DK, (hd + 1) * GLA_DK)

    def dv_of(hd):
        return slice(hd * GLA_DV, (hd + 1) * GLA_DV)

    vcs = {(c, hd): h_ref[rows_of(c), c_v + hd * GLA_DV:c_v + (hd + 1) * GLA_DV].astype(BF16) for c, hd in units}
    raw = {(c, hd): _dot_nt(qe[rows_of(c), dk_of(hd)], ke[rows_of(c), dk_of(hd)]) for c, hd in units}
    upd = {(c, hd): _dot(kd[rows_of(c), dk_of(hd)].T.astype(BF16), vcs[c, hd]) for c, hd in units}
    att = {u: jnp.where(causal, raw[u], 0.0).astype(BF16) for u in units}
    intra = {u: _dot(att[u], vcs[u]) for u in units}
    state = [state_ref[hd] for hd in range(GLA_HEADS)]
    for c in range(n_chunk):
        for hd in range(GLA_HEADS):
            oc_ref[rows_of(c), dv_of(hd)] = intra[c, hd] + _dot(qe[rows_of(c), dk_of(hd)], state[hd].astype(BF16))
            state[hd] = state[hd] * dec_t[dk_of(hd), c:c + 1] + upd[c, hd]
    for hd in range(GLA_HEADS):
        state_ref[hd] = state[hd]

    gon = gon_ref[...]
    for sb in range(MIX_TILE // MIX_SUB):
        rows = slice(sb * MIX_SUB, (sb + 1) * MIX_SUB)
        parts = []
        for hd in range(GLA_HEADS):
            o = oc_ref[rows, dv_of(hd)]
            o = o * lax.rsqrt(jnp.mean(o * o, axis=-1, keepdims=True) + RMS_EPS) * gon
            r = h_ref[rows, c_r + hd * GLA_DV:c_r + (hd + 1) * GLA_DV]
            parts.append((o * _silu(r)).astype(BF16))
        parts += _mem_attention(h_ref[rows, c_qm:c_qm + MEM_W], kv_ref)
        y = _dot(jnp.concatenate(parts, axis=-1), wout_ref[0])
        o_ref[0, rows, :] = _layer_norm(ALPHA * x[rows, :] + y, lng_ref[...], lnb_ref[...])


def _gla_layer(x, kv, w_in, w_gate2, b_gate, g_onorm, wout, layer, ln_g, ln_b):
    b, s, _ = x.shape
    n_main = 2 * GLA_QK + 2 * GLA_V
    win = jnp.concatenate(
        [w_in[:, :n_main], w_in[:, n_main + GLA_GATE_RANK:],
         jnp.pad(w_in[:, n_main:n_main + GLA_GATE_RANK], ((0, 0), (0, GLA_GATE_PAD - GLA_GATE_RANK)))],
        axis=1).astype(BF16)
    wg2 = jnp.pad(w_gate2, ((0, GLA_GATE_PAD - GLA_GATE_RANK), (0, 0))).astype(BF16)
    return pl.pallas_call(
        _gla_kernel,
        grid=(b, s // MIX_TILE),
        in_specs=[
            pl.BlockSpec((1, MIX_TILE, D_MODEL), lambda i, j: (i, j, 0)),
            _resident(win.shape),
            _resident(wg2.shape),
            _resident((1, GLA_QK)),
            _resident((1, GLA_DV)),
            pl.BlockSpec((1, N_MEM, 2 * MEM_W), lambda i, j: (i, 0, 0)),
            _layer_slab(wout.shape, layer),
            _resident((1, D_MODEL)),
            _resident((1, D_MODEL)),
        ],
        out_specs=pl.BlockSpec((1, MIX_TILE, D_MODEL), lambda i, j: (i, j, 0)),
        out_shape=jax.ShapeDtypeStruct((b, s, D_MODEL), F32),
        scratch_shapes=[
            pltpu.VMEM((GLA_HEADS, GLA_DK, GLA_DV), F32),
            pltpu.VMEM((MIX_TILE, GLA_IN_PAD), F32),
            pltpu.VMEM((MIX_TILE, GLA_V), F32),
        ],
        compiler_params=pltpu.CompilerParams(
            dimension_semantics=("arbitrary", "arbitrary"), vmem_limit_bytes=VMEM_LIMIT),
        name="gla_mixer",
    )(x, win, wg2, b_gate.reshape(1, GLA_QK), g_onorm.reshape(1, GLA_DV), kv,
      wout, ln_g.reshape(1, D_MODEL), ln_b.reshape(1, D_MODEL))


CONV_SEG = MIX_TILE // 8
CONV_HALO_V = 32
CONV_HALO_ROWS = CONV_HALO_V * 8
CONV_GROUP = 256
CONV_NGROUP = CONV_CH // CONV_GROUP
CONV_NSLAB = CONV_CH // LANES


def _conv_kernel(x_ref, win_ref, w16_ref, bdw_ref, cg_ref, cb_ref, kv_ref, wout_ref, lng_ref, lnb_ref,
                 o_ref, ext_ref, exta_ref, extb_ref, cvp_ref, cv_ref):
    @pl.when(pl.program_id(1) == 0)
    def _():
        ext_ref[:, 0:CONV_HALO_ROWS, :] = jnp.zeros((CONV_NSLAB, CONV_HALO_ROWS, LANES), F32)

    x = x_ref[0]
    xb = x.astype(BF16)
    off = CONV_HALO_V - (CONV_WIDTH - 1)
    ext_rows = CONV_HALO_ROWS + MIX_TILE
    slabs_per_group = CONV_GROUP // LANES

    def project(g):
        lo = g * CONV_GROUP
        value = _dot(xb, win_ref[0, :, lo:lo + CONV_GROUP])
        gate = _dot(xb, win_ref[0, :, CONV_CH + lo:CONV_CH + lo + CONV_GROUP])
        u = value * jax.nn.sigmoid(gate)
        for half in range(slabs_per_group):
            sl = g * slabs_per_group + half
            us = u[:, half * LANES:(half + 1) * LANES]
            for s in range(8):
                seg = us[s * CONV_SEG:(s + 1) * CONV_SEG, :]
                ext_ref[sl, pl.ds(CONV_HALO_ROWS + s, CONV_SEG, stride=8), :] = seg
                if s < 7:
                    ext_ref[sl, pl.ds(s + 1, CONV_HALO_V, stride=8), :] = seg[CONV_SEG - CONV_HALO_V:, :]
            exta_ref[sl] = ext_ref[sl].astype(BF16)
            extb_ref[sl, 0:ext_rows - 16, :] = ext_ref[sl, 8:ext_rows - 8, :].astype(BF16)
            ext_ref[sl, pl.ds(0, CONV_HALO_V, stride=8), :] = us[MIX_TILE - CONV_HALO_V:, :]

    def convolve(g):
        for half in range(slabs_per_group):
            sl = g * slabs_per_group + half
            cols = slice(sl * LANES, (sl + 1) * LANES)
            bias = bdw_ref[:, cols]
            for k in range(MIX_TILE // 16):
                acc = None
                for j in range(CONV_WIDTH):
                    m = 2 * k + j + off
                    src = exta_ref if m % 2 == 0 else extb_ref
                    r0 = (m // 2) * 16
                    term = src[sl, r0:r0 + 16, :].astype(F32) * w16_ref[j, sl].astype(F32)
                    acc = term if acc is None else acc + term
                cvp_ref[sl, k * 16:(k + 1) * 16, :] = acc + bias
            for s in range(8):
                cv_ref[s * CONV_SEG:(s + 1) * CONV_SEG, cols] = cvp_ref[sl, pl.ds(s, CONV_SEG, stride=8), :]

    project(0)
    for g in range(1, CONV_NGROUP):
        project(g)
        convolve(g - 1)
    mem = _mem_attention(_dot(xb, win_ref[0, :, 2 * CONV_CH:]), kv_ref)
    convolve(CONV_NGROUP - 1)

    for sb in range(MIX_TILE // MIX_SUB):
        rows = slice(sb * MIX_SUB, (sb + 1) * MIX_SUB)
        y = _silu(_layer_norm(cv_ref[rows, :], cg_ref[...], cb_ref[...])).astype(BF16)
        parts = [y] + [m[rows, :] for m in mem]
        out = _dot(jnp.concatenate(parts, axis=-1), wout_ref[0])
        o_ref[0, rows, :] = _layer_norm(ALPHA * x[rows, :] + out, lng_ref[...], lnb_ref[...])


def _conv_layer(x, kv, win, w_dw, b_dw, c_g, c_b, wout, layer, ln_g, ln_b):
    b, s, _ = x.shape
    w16 = jnp.broadcast_to(w_dw.astype(BF16).reshape(CONV_WIDTH, CONV_NSLAB, 1, LANES),
                           (CONV_WIDTH, CONV_NSLAB, 16, LANES))
    return pl.pallas_call(
        _conv_kernel,
        grid=(b, s // MIX_TILE),
        in_specs=[
            pl.BlockSpec((1, MIX_TILE, D_MODEL), lambda i, j: (i, j, 0)),
            _layer_slab(win.shape, layer),
            _resident((CONV_WIDTH, CONV_NSLAB, 16, LANES)),
            _resident((1, CONV_CH)),
            _resident((1, CONV_CH)),
            _resident((1, CONV_CH)),
            pl.BlockSpec((1, N_MEM, 2 * MEM_W), lambda i, j: (i, 0, 0)),
            _layer_slab(wout.shape, layer),
            _resident((1, D_MODEL)),
            _resident((1, D_MODEL)),
        ],
        out_specs=pl.BlockSpec((1, MIX_TILE, D_MODEL), lambda i, j: (i, j, 0)),
        out_shape=jax.ShapeDtypeStruct((b, s, D_MODEL), F32),
        scratch_shapes=[
            pltpu.VMEM((CONV_NSLAB, CONV_HALO_ROWS + MIX_TILE, LANES), F32),
            pltpu.VMEM((CONV_NSLAB, CONV_HALO_ROWS + MIX_TILE, LANES), BF16),
            pltpu.VMEM((CONV_NSLAB, CONV_HALO_ROWS + MIX_TILE, LANES), BF16),
            pltpu.VMEM((CONV_NSLAB, MIX_TILE, LANES), F32),
            pltpu.VMEM((MIX_TILE, CONV_CH), F32),
        ],
        compiler_params=pltpu.CompilerParams(
            dimension_semantics=("arbitrary", "arbitrary"), vmem_limit_bytes=VMEM_LIMIT),
        name="conv_mixer",
    )(x, win, w16, b_dw.reshape(1, CONV_CH), c_g.reshape(1, CONV_CH),
      c_b.reshape(1, CONV_CH), kv, wout, ln_g.reshape(1, D_MODEL), ln_b.reshape(1, D_MODEL))


def kernel(x, mem, ln_g, ln_b, ffn1_w_gu, ffn1_w_down, ffn2_w_gu, ffn2_w_down, mem_w_kv, gla_w_in, gla_w_gate2, gla_b_gate, gla_g_onorm, gla_w_out, conv_w_in, conv_w_dw, conv_b_dw, conv_ln_g, conv_ln_b, conv_w_out):
    b, s, d = x.shape
    kv = _mem_kv(mem, mem_w_kv)
    ffn_w = [(ffn1_w_gu.astype(BF16), ffn1_w_down.astype(BF16)), (ffn2_w_gu.astype(BF16), ffn2_w_down.astype(BF16))]
    gla_wout = gla_w_out.astype(BF16)
    conv_win = conv_w_in.astype(BF16)
    conv_wout = conv_w_out.astype(BF16)
    for i in range(DEPTH):
        j = i // 2
        x = _ffn_ln(x.reshape(b * s, d), ffn_w[0][0], ffn_w[0][1], i, ln_g[i, 0], ln_b[i, 0]).reshape(b, s, d)
        if i % 2 == 0:
            x = _gla_layer(x, kv[i], gla_w_in[j], gla_w_gate2[j], gla_b_gate[j], gla_g_onorm[j],
                           gla_wout, j, ln_g[i, 1], ln_b[i, 1])
        else:
            x = _conv_layer(x, kv[i], conv_win, conv_w_dw[j], conv_b_dw[j], conv_ln_g[j], conv_ln_b[j],
                            conv_wout, j, ln_g[i, 1], ln_b[i, 1])
        x = _ffn_ln(x.reshape(b * s, d), ffn_w[1][0], ffn_w[1][1], i, ln_g[i, 2], ln_b[i, 2]).reshape(b, s, d)
    return x
```

```python
import jax
import jax.numpy as jnp
from jax import lax
from jax.experimental import pallas as pl
from jax.experimental.pallas import tpu as pltpu

F32 = jnp.float32
BF16 = jnp.bfloat16

D_MODEL = 1024
DEPTH = 4
N_MEM = 256
GLA_HEADS = 4
GLA_DK = 128
GLA_DV = 256
GLA_QK = GLA_HEADS * GLA_DK
GLA_V = GLA_HEADS * GLA_DV
GLA_GATE_RANK = 16
GLA_GATE_NORM = 16.0
GLA_CHUNK = 64
CONV_CH = D_MODEL
CONV_WIDTH = 31
MEM_HEADS = 4
MEM_DH = 128
MEM_W = MEM_HEADS * MEM_DH
D_FF = 2816
ALPHA = (2 * DEPTH) ** 0.25
LN_EPS = 1e-5
RMS_EPS = 1e-6

LANES = 128
FFN_TILE = 512
FFN_CHUNK = 256
FFN_NCHUNK = D_FF // FFN_CHUNK
MIX_TILE = 512
MIX_SUB = 256
VMEM_LIMIT = 56 * 1024 * 1024

GLA_GATE_PAD = LANES
GLA_IN_PAD = 2 * GLA_QK + 2 * GLA_V + MEM_W + GLA_GATE_PAD


def _resident(shape):
    nd = len(shape)
    return pl.BlockSpec(shape, lambda *_: (0,) * nd, pipeline_mode=pl.Buffered(1))


def _layer_slab(shape, layer):
    nd = len(shape)
    return pl.BlockSpec((1,) + tuple(shape[1:]), lambda *_: (layer,) + (0,) * (nd - 1),
                        pipeline_mode=pl.Buffered(1))


def _layer_norm(z, g, b):
    mu = jnp.mean(z, axis=-1, keepdims=True)
    zc = z - mu
    var = jnp.mean(zc * zc, axis=-1, keepdims=True)
    return zc * lax.rsqrt(var + LN_EPS) * g + b


def _silu(t):
    return t * jax.nn.sigmoid(t)


def _dot(a, b):
    return jnp.dot(a, b, preferred_element_type=F32)


def _dot_nt(a, b):
    return lax.dot_general(a, b, (((1,), (1,)), ((), ())), preferred_element_type=F32)


def _ffn_kernel(x_ref, wgu_ref, wd_ref, g_ref, b_ref, o_ref):
    x = x_ref[...]
    xb = x.astype(BF16)
    acc = jnp.zeros((FFN_TILE, D_MODEL), F32)
    for c in range(FFN_NCHUNK):
        lo = c * FFN_CHUNK
        gate = _dot(xb, wgu_ref[0, :, lo:lo + FFN_CHUNK].astype(BF16))
        up = _dot(xb, wgu_ref[0, :, D_FF + lo:D_FF + lo + FFN_CHUNK].astype(BF16))
        acc = acc + _dot((_silu(gate) * up).astype(BF16), wd_ref[0, lo:lo + FFN_CHUNK, :].astype(BF16))
    o_ref[...] = _layer_norm(ALPHA * x + 0.5 * acc, g_ref[...], b_ref[...])


def _ffn_ln(x2d, wgu, wd, layer, g, b):
    t = x2d.shape[0]
    return pl.pallas_call(
        _ffn_kernel,
        grid=(t // FFN_TILE,),
        in_specs=[
            pl.BlockSpec((FFN_TILE, D_MODEL), lambda i: (i, 0)),
            _layer_slab(wgu.shape, layer),
            _layer_slab(wd.shape, layer),
            _resident((1, D_MODEL)),
            _resident((1, D_MODEL)),
        ],
        out_specs=pl.BlockSpec((FFN_TILE, D_MODEL), lambda i: (i, 0)),
        out_shape=jax.ShapeDtypeStruct((t, D_MODEL), F32),
        compiler_params=pltpu.CompilerParams(
            dimension_semantics=("arbitrary",), vmem_limit_bytes=VMEM_LIMIT),
        name="ffn_ln",
    )(x2d, wgu, wd, g.reshape(1, D_MODEL), b.reshape(1, D_MODEL))


def _chunk_cumsum(t):
    pos = lax.broadcasted_iota(jnp.int32, t.shape, 0) % GLA_CHUNK
    step = 1
    while step < GLA_CHUNK:
        t = t + jnp.where(pos >= step, pltpu.roll(t, step, 0), 0.0)
        step *= 2
    return t


def _kv_kernel(mem_ref, w_ref, o_ref):
    o_ref[0] = _dot(mem_ref[...], w_ref[0]).astype(BF16)


def _mem_kv(mem, mem_w_kv):
    b = mem.shape[0]
    mem2d = mem.reshape(b * N_MEM, D_MODEL).astype(BF16)
    kv = pl.pallas_call(
        _kv_kernel,
        grid=(DEPTH,),
        in_specs=[
            _resident(mem2d.shape),
            pl.BlockSpec((1, D_MODEL, 2 * MEM_W), lambda i: (i, 0, 0)),
        ],
        out_specs=pl.BlockSpec((1, b * N_MEM, 2 * MEM_W), lambda i: (i, 0, 0)),
        out_shape=jax.ShapeDtypeStruct((DEPTH, b * N_MEM, 2 * MEM_W), BF16),
        compiler_params=pltpu.CompilerParams(
            dimension_semantics=("arbitrary",), vmem_limit_bytes=VMEM_LIMIT),
        name="mem_kv",
    )(mem2d, mem_w_kv.astype(BF16))
    return kv.reshape(DEPTH, b, N_MEM, 2 * MEM_W)


def _mem_attention(qm, kv_ref):
    outs = []
    for hd in range(MEM_HEADS):
        lo = hd * MEM_DH
        q = qm[:, lo:lo + MEM_DH].astype(BF16)
        k = kv_ref[0, :, lo:lo + MEM_DH]
        v = kv_ref[0, :, MEM_W + lo:MEM_W + lo + MEM_DH]
        s = _dot_nt(q, k) * (MEM_DH ** -0.5)
        e = jnp.exp(s - jnp.max(s, axis=-1, keepdims=True))
        p = e * (1.0 / jnp.sum(e, axis=-1, keepdims=True))
        outs.append(_dot(p.astype(BF16), v).astype(BF16))
    return outs


def _gla_kernel(x_ref, win_ref, wg2_ref, bg_ref, gon_ref, kv_ref, wout_ref, lng_ref, lnb_ref,
                o_ref, state_ref, h_ref, oc_ref):
    @pl.when(pl.program_id(1) == 0)
    def _():
        state_ref[...] = jnp.zeros_like(state_ref)

    x = x_ref[0]
    h_ref[...] = _dot(x.astype(BF16), win_ref[...])
    c_q, c_k, c_v, c_r = 0, GLA_QK, 2 * GLA_QK, 2 * GLA_QK + GLA_V
    c_qm = c_r + GLA_V
    c_a = c_qm + MEM_W

    z = _dot(h_ref[:, c_a:c_a + GLA_GATE_PAD].astype(BF16), wg2_ref[...]) + bg_ref[...]
    lg = (jnp.minimum(z, 0.0) - jnp.log1p(jnp.exp(-jnp.abs(z)))) * (1.0 / GLA_GATE_NORM)
    bc = _chunk_cumsum(lg)
    n_chunk = MIX_TILE // GLA_CHUNK
    last = [bc[(c + 1) * GLA_CHUNK - 1:(c + 1) * GLA_CHUNK, :] for c in range(n_chunk)]
    bl = jnp.concatenate([jnp.broadcast_to(t, (GLA_CHUNK, GLA_QK)) for t in last], axis=0)
    q = h_ref[:, c_q:c_q + GLA_QK]
    k = h_ref[:, c_k:c_k + GLA_QK]
    qe = ((q * (GLA_DK ** -0.5)) * jnp.exp(bc)).astype(BF16)
    ke = (k * jnp.exp(-bc)).astype(BF16)
    kd = k * jnp.exp(bl - bc)
    dec_t = jnp.exp(jnp.concatenate(last, axis=0)).T

    row = lax.broadcasted_iota(jnp.int32, (GLA_CHUNK, GLA_CHUNK), 0)
    col = lax.broadcasted_iota(jnp.int32, (GLA_CHUNK, GLA_CHUNK), 1)
    causal = row >= col
    units = [(c, hd) for c in range(n_chunk) for hd in range(GLA_HEADS)]

    def rows_of(c):
        return slice(c * GLA_CHUNK, (c + 1) * GLA_CHUNK)

    def dk_of(hd):
        return slice(hd * GLA_DK, (hd + 1) * GLA_DK)

    def dv_of(hd):
        return slice(hd * GLA_DV, (hd + 1) * GLA_DV)

    vcs = {(c, hd): h_ref[rows_of(c), c_v + hd * GLA_DV:c_v + (hd + 1) * GLA_DV].astype(BF16) for c, hd in units}
    raw = {(c, hd): _dot_nt(qe[rows_of(c), dk_of(hd)], ke[rows_of(c), dk_of(hd)]) for c, hd in units}
    upd = {(c, hd): _dot(kd[rows_of(c), dk_of(hd)].T.astype(BF16), vcs[c, hd]) for c, hd in units}
    att = {u: jnp.where(causal, raw[u], 0.0).astype(BF16) for u in units}
    intra = {u: _dot(att[u], vcs[u]) for u in units}
    state = [state_ref[hd] for hd in range(GLA_HEADS)]
    for c in range(n_chunk):
        for hd in range(GLA_HEADS):
            oc_ref[rows_of(c), dv_of(hd)] = intra[c, hd] + _dot(qe[rows_of(c), dk_of(hd)], state[hd].astype(BF16))
            state[hd] = state[hd] * dec_t[dk_of(hd), c:c + 1] + upd[c, hd]
    for hd in range(GLA_HEADS):
        state_ref[hd] = state[hd]

    gon = gon_ref[...]
    for sb in range(MIX_TILE // MIX_SUB):
        rows = slice(sb * MIX_SUB, (sb + 1) * MIX_SUB)
        parts = []
        for hd in range(GLA_HEADS):
            o = oc_ref[rows, dv_of(hd)]
            o = o * lax.rsqrt(jnp.mean(o * o, axis=-1, keepdims=True) + RMS_EPS) * gon
            r = h_ref[rows, c_r + hd * GLA_DV:c_r + (hd + 1) * GLA_DV]
            parts.append((o * _silu(r)).astype(BF16))
        parts += _mem_attention(h_ref[rows, c_qm:c_qm + MEM_W], kv_ref)
        y = _dot(jnp.concatenate(parts, axis=-1), wout_ref[0])
        o_ref[0, rows, :] = _layer_norm(ALPHA * x[rows, :] + y, lng_ref[...], lnb_ref[...])


def _gla_layer(x, kv, w_in, w_gate2, b_gate, g_onorm, wout, layer, ln_g, ln_b):
    b, s, _ = x.shape
    n_main = 2 * GLA_QK + 2 * GLA_V
    win = jnp.concatenate(
        [w_in[:, :n_main], w_in[:, n_main + GLA_GATE_RANK:],
         jnp.pad(w_in[:, n_main:n_main + GLA_GATE_RANK], ((0, 0), (0, GLA_GATE_PAD - GLA_GATE_RANK)))],
        axis=1).astype(BF16)
    wg2 = jnp.pad(w_gate2, ((0, GLA_GATE_PAD - GLA_GATE_RANK), (0, 0))).astype(BF16)
    return pl.pallas_call(
        _gla_kernel,
        grid=(b, s // MIX_TILE),
        in_specs=[
            pl.BlockSpec((1, MIX_TILE, D_MODEL), lambda i, j: (i, j, 0)),
            _resident(win.shape),
            _resident(wg2.shape),
            _resident((1, GLA_QK)),
            _resident((1, GLA_DV)),
            pl.BlockSpec((1, N_MEM, 2 * MEM_W), lambda i, j: (i, 0, 0)),
            _layer_slab(wout.shape, layer),
            _resident((1, D_MODEL)),
            _resident((1, D_MODEL)),
        ],
        out_specs=pl.BlockSpec((1, MIX_TILE, D_MODEL), lambda i, j: (i, j, 0)),
        out_shape=jax.ShapeDtypeStruct((b, s, D_MODEL), F32),
        scratch_shapes=[
            pltpu.VMEM((GLA_HEADS, GLA_DK, GLA_DV), F32),
            pltpu.VMEM((MIX_TILE, GLA_IN_PAD), F32),
            pltpu.VMEM((MIX_TILE, GLA_V), F32),
        ],
        compiler_params=pltpu.CompilerParams(
            dimension_semantics=("arbitrary", "arbitrary"), vmem_limit_bytes=VMEM_LIMIT),
        name="gla_mixer",
    )(x, win, wg2, b_gate.reshape(1, GLA_QK), g_onorm.reshape(1, GLA_DV), kv,
      wout, ln_g.reshape(1, D_MODEL), ln_b.reshape(1, D_MODEL))


CONV_SEG = MIX_TILE // 8
CONV_HALO_V = 32
CONV_HALO_ROWS = CONV_HALO_V * 8
CONV_GROUP = 256
CONV_NGROUP = CONV_CH // CONV_GROUP
CONV_NSLAB = CONV_CH // LANES


def _conv_kernel(x_ref, win_ref, w16_ref, bdw_ref, cg_ref, cb_ref, kv_ref, wout_ref, lng_ref, lnb_ref,
                 o_ref, ext_ref, exta_ref, extb_ref, cvp_ref, cv_ref):
    @pl.when(pl.program_id(1) == 0)
    def _():
        ext_ref[:, 0:CONV_HALO_ROWS, :] = jnp.zeros((CONV_NSLAB, CONV_HALO_ROWS, LANES), F32)

    x = x_ref[0]
    xb = x.astype(BF16)
    off = CONV_HALO_V - (CONV_WIDTH - 1)
    ext_rows = CONV_HALO_ROWS + MIX_TILE
    slabs_per_group = CONV_GROUP // LANES

    def project(g):
        lo = g * CONV_GROUP
        value = _dot(xb, win_ref[0, :, lo:lo + CONV_GROUP])
        gate = _dot(xb, win_ref[0, :, CONV_CH + lo:CONV_CH + lo + CONV_GROUP])
        u = value * jax.nn.sigmoid(gate)
        for half in range(slabs_per_group):
            sl = g * slabs_per_group + half
            us = u[:, half * LANES:(half + 1) * LANES]
            for s in range(8):
                seg = us[s * CONV_SEG:(s + 1) * CONV_SEG, :]
                ext_ref[sl, pl.ds(CONV_HALO_ROWS + s, CONV_SEG, stride=8), :] = seg
                if s < 7:
                    ext_ref[sl, pl.ds(s + 1, CONV_HALO_V, stride=8), :] = seg[CONV_SEG - CONV_HALO_V:, :]
            exta_ref[sl] = ext_ref[sl].astype(BF16)
            extb_ref[sl, 0:ext_rows - 16, :] = ext_ref[sl, 8:ext_rows - 8, :].astype(BF16)
            ext_ref[sl, pl.ds(0, CONV_HALO_V, stride=8), :] = us[MIX_TILE - CONV_HALO_V:, :]

    def convolve(g):
        for half in range(slabs_per_group):
            sl = g * slabs_per_group + half
            cols = slice(sl * LANES, (sl + 1) * LANES)
            bias = bdw_ref[:, cols]
            for k in range(MIX_TILE // 16):
                acc = None
                for j in range(CONV_WIDTH):
                    m = 2 * k + j + off
                    src = exta_ref if m % 2 == 0 else extb_ref
                    r0 = (m // 2) * 16
                    term = src[sl, r0:r0 + 16, :].astype(F32) * w16_ref[j, sl].astype(F32)
                    acc = term if acc is None else acc + term
                cvp_ref[sl, k * 16:(k + 1) * 16, :] = acc + bias
            for s in range(8):
                cv_ref[s * CONV_SEG:(s + 1) * CONV_SEG, cols] = cvp_ref[sl, pl.ds(s, CONV_SEG, stride=8), :]

    project(0)
    for g in range(1, CONV_NGROUP):
        project(g)
        convolve(g - 1)
    mem = _mem_attention(_dot(xb, win_ref[0, :, 2 * CONV_CH:]), kv_ref)
    convolve(CONV_NGROUP - 1)

    for sb in range(MIX_TILE // MIX_SUB):
        rows = slice(sb * MIX_SUB, (sb + 1) * MIX_SUB)
        y = _silu(_layer_norm(cv_ref[rows, :], cg_ref[...], cb_ref[...])).astype(BF16)
        parts = [y] + [m[rows, :] for m in mem]
        out = _dot(jnp.concatenate(parts, axis=-1), wout_ref[0])
        o_ref[0, rows, :] = _layer_norm(ALPHA * x[rows, :] + out, lng_ref[...], lnb_ref[...])


def _conv_layer(x, kv, win, w_dw, b_dw, c_g, c_b, wout, layer, ln_g, ln_b):
    b, s, _ = x.shape
    w16 = jnp.broadcast_to(w_dw.astype(BF16).reshape(CONV_WIDTH, CONV_NSLAB, 1, LANES),
                           (CONV_WIDTH, CONV_NSLAB, 16, LANES))
    return pl.pallas_call(
        _conv_kernel,
        grid=(b, s // MIX_TILE),
        in_specs=[
            pl.BlockSpec((1, MIX_TILE, D_MODEL), lambda i, j: (i, j, 0)),
            _layer_slab(win.shape, layer),
            _resident((CONV_WIDTH, CONV_NSLAB, 16, LANES)),
            _resident((1, CONV_CH)),
            _resident((1, CONV_CH)),
            _resident((1, CONV_CH)),
            pl.BlockSpec((1, N_MEM, 2 * MEM_W), lambda i, j: (i, 0, 0)),
            _layer_slab(wout.shape, layer),
            _resident((1, D_MODEL)),
            _resident((1, D_MODEL)),
        ],
        out_specs=pl.BlockSpec((1, MIX_TILE, D_MODEL), lambda i, j: (i, j, 0)),
        out_shape=jax.ShapeDtypeStruct((b, s, D_MODEL), F32),
        scratch_shapes=[
            pltpu.VMEM((CONV_NSLAB, CONV_HALO_ROWS + MIX_TILE, LANES), F32),
            pltpu.VMEM((CONV_NSLAB, CONV_HALO_ROWS + MIX_TILE, LANES), BF16),
            pltpu.VMEM((CONV_NSLAB, CONV_HALO_ROWS + MIX_TILE, LANES), BF16),
            pltpu.VMEM((CONV_NSLAB, MIX_TILE, LANES), F32),
            pltpu.VMEM((MIX_TILE, CONV_CH), F32),
        ],
        compiler_params=pltpu.CompilerParams(
            dimension_semantics=("arbitrary", "arbitrary"), vmem_limit_bytes=VMEM_LIMIT),
        name="conv_mixer",
    )(x, win, w16, b_dw.reshape(1, CONV_CH), c_g.reshape(1, CONV_CH),
      c_b.reshape(1, CONV_CH), kv, wout, ln_g.reshape(1, D_MODEL), ln_b.reshape(1, D_MODEL))


def kernel(x, mem, ln_g, ln_b, ffn1_w_gu, ffn1_w_down, ffn2_w_gu, ffn2_w_down, mem_w_kv, gla_w_in, gla_w_gate2, gla_b_gate, gla_g_onorm, gla_w_out, conv_w_in, conv_w_dw, conv_b_dw, conv_ln_g, conv_ln_b, conv_w_out):
    b, s, d = x.shape
    kv = _mem_kv(mem, mem_w_kv)
    ffn_w = [(ffn1_w_gu, ffn1_w_down), (ffn2_w_gu, ffn2_w_down)]
    gla_wout = gla_w_out.astype(BF16)
    conv_win = conv_w_in.astype(BF16)
    conv_wout = conv_w_out.astype(BF16)
    for i in range(DEPTH):
        j = i // 2
        x = _ffn_ln(x.reshape(b * s, d), ffn_w[0][0], ffn_w[0][1], i, ln_g[i, 0], ln_b[i, 0]).reshape(b, s, d)
        if i % 2 == 0:
            x = _gla_layer(x, kv[i], gla_w_in[j], gla_w_gate2[j], gla_b_gate[j], gla_g_onorm[j],
                           gla_wout, j, ln_g[i, 1], ln_b[i, 1])
        else:
            x = _conv_layer(x, kv[i], conv_win, conv_w_dw[j], conv_b_dw[j], conv_ln_g[j], conv_ln_b[j],
                            conv_wout, j, ln_g[i, 1], ln_b[i, 1])
        x = _ffn_ln(x.reshape(b * s, d), ffn_w[1][0], ffn_w[1][1], i, ln_g[i, 2], ln_b[i, 2]).reshape(b, s, d)
    return x
```

```python
import jax
import jax.numpy as jnp
from jax import lax
from jax.experimental import pallas as pl
from jax.experimental.pallas import tpu as pltpu

F32 = jnp.float32
BF16 = jnp.bfloat16

D_MODEL = 1024
DEPTH = 4
N_MEM = 256
GLA_HEADS = 4
GLA_DK = 128
GLA_DV = 256
GLA_QK = GLA_HEADS * GLA_DK
GLA_V = GLA_HEADS * GLA_DV
GLA_GATE_RANK = 16
GLA_GATE_NORM = 16.0
GLA_CHUNK = 64
CONV_CH = D_MODEL
CONV_WIDTH = 31
MEM_HEADS = 4
MEM_DH = 128
MEM_W = MEM_HEADS * MEM_DH
D_FF = 2816
ALPHA = (2 * DEPTH) ** 0.25
LN_EPS = 1e-5
RMS_EPS = 1e-6

LANES = 128
FFN_TILE = 512
FFN_CHUNK = 256
FFN_NCHUNK = D_FF // FFN_CHUNK
MIX_TILE = 512
V7X_VMEM_BYTES = 64 * 1024 * 1024
VMEM_LIMIT = V7X_VMEM_BYTES * 7 // 8

GLA_GATE_PAD = LANES
GLA_IN_PAD = 2 * GLA_QK + 2 * GLA_V + MEM_W + GLA_GATE_PAD


def _resident(shape):
    nd = len(shape)
    return pl.BlockSpec(shape, lambda *_: (0,) * nd, pipeline_mode=pl.Buffered(1))


def _layer_slab(shape, layer):
    nd = len(shape)
    return pl.BlockSpec((1,) + tuple(shape[1:]), lambda *_: (layer,) + (0,) * (nd - 1),
                        pipeline_mode=pl.Buffered(1))


def _layer_norm(z, g, b):
    mu = jnp.mean(z, axis=-1, keepdims=True)
    zc = z - mu
    var = jnp.mean(zc * zc, axis=-1, keepdims=True)
    return zc * lax.rsqrt(var + LN_EPS) * g + b


def _silu(t):
    return t * jax.nn.sigmoid(t)


def _dot(a, b):
    return jnp.dot(a, b, preferred_element_type=F32)


def _dot_nt(a, b):
    return lax.dot_general(a, b, (((1,), (1,)), ((), ())), preferred_element_type=F32)


def _ffn_kernel(x_ref, wgu_ref, wd_ref, g_ref, b_ref, o_ref):
    x = x_ref[...]
    xb = x.astype(BF16)
    acc = jnp.zeros((FFN_TILE, D_MODEL), F32)
    for c in range(FFN_NCHUNK):
        lo = c * FFN_CHUNK
        gate = _dot(xb, wgu_ref[0, :, lo:lo + FFN_CHUNK].astype(BF16))
        up = _dot(xb, wgu_ref[0, :, D_FF + lo:D_FF + lo + FFN_CHUNK].astype(BF16))
        acc = acc + _dot((_silu(gate) * up).astype(BF16), wd_ref[0, lo:lo + FFN_CHUNK, :].astype(BF16))
    o_ref[...] = _layer_norm(ALPHA * x + 0.5 * acc, g_ref[...], b_ref[...])


def _ffn_ln(x2d, wgu, wd, layer, g, b):
    t = x2d.shape[0]
    assert t % FFN_TILE == 0
    return pl.pallas_call(
        _ffn_kernel,
        grid=(t // FFN_TILE,),
        in_specs=[
            pl.BlockSpec((FFN_TILE, D_MODEL), lambda i: (i, 0)),
            _layer_slab(wgu.shape, layer),
            _layer_slab(wd.shape, layer),
            _resident((1, D_MODEL)),
            _resident((1, D_MODEL)),
        ],
        out_specs=pl.BlockSpec((FFN_TILE, D_MODEL), lambda i: (i, 0)),
        out_shape=jax.ShapeDtypeStruct((t, D_MODEL), F32),
        compiler_params=pltpu.CompilerParams(
            dimension_semantics=("arbitrary",), vmem_limit_bytes=VMEM_LIMIT),
        name="ffn_ln",
    )(x2d, wgu, wd, g.reshape(1, D_MODEL), b.reshape(1, D_MODEL))


def _chunk_cumsum(t):
    pos = lax.broadcasted_iota(jnp.int32, t.shape, 0) % GLA_CHUNK
    step = 1
    while step < GLA_CHUNK:
        t = t + jnp.where(pos >= step, pltpu.roll(t, step, 0), 0.0)
        step *= 2
    return t


def _kv_kernel(mem_ref, w_ref, o_ref):
    o_ref[0] = _dot(mem_ref[...], w_ref[0]).astype(BF16)


def _mem_kv(mem, mem_w_kv):
    b = mem.shape[0]
    mem2d = mem.reshape(b * N_MEM, D_MODEL).astype(BF16)
    kv = pl.pallas_call(
        _kv_kernel,
        grid=(DEPTH,),
        in_specs=[
            _resident(mem2d.shape),
            pl.BlockSpec((1, D_MODEL, 2 * MEM_W), lambda i: (i, 0, 0)),
        ],
        out_specs=pl.BlockSpec((1, b * N_MEM, 2 * MEM_W), lambda i: (i, 0, 0)),
        out_shape=jax.ShapeDtypeStruct((DEPTH, b * N_MEM, 2 * MEM_W), BF16),
        compiler_params=pltpu.CompilerParams(
            dimension_semantics=("arbitrary",), vmem_limit_bytes=VMEM_LIMIT),
        name="mem_kv",
    )(mem2d, mem_w_kv.astype(BF16))
    return kv.reshape(DEPTH, b, N_MEM, 2 * MEM_W)


def _mem_attention(qm, kv_ref):
    outs = []
    for hd in range(MEM_HEADS):
        lo = hd * MEM_DH
        q = qm[:, lo:lo + MEM_DH].astype(BF16)
        k = kv_ref[0, 0, :, lo:lo + MEM_DH]
        v = kv_ref[0, 0, :, MEM_W + lo:MEM_W + lo + MEM_DH]
        s = _dot_nt(q, k) * (MEM_DH ** -0.5)
        e = jnp.exp(s - jnp.max(s, axis=-1, keepdims=True))
        p = e * (1.0 / jnp.sum(e, axis=-1, keepdims=True))
        outs.append(_dot(p.astype(BF16), v).astype(BF16))
    return outs


def _gla_kernel(x_ref, win_ref, wg2_ref, bg_ref, gon_ref, kv_ref, wout_ref, lng_ref, lnb_ref,
                o_ref, state_ref, h_ref, oc_ref):
    @pl.when(pl.program_id(1) == 0)
    def _():
        state_ref[...] = jnp.zeros_like(state_ref)

    x = x_ref[0]
    h_ref[...] = _dot(x.astype(BF16), win_ref[...])
    c_q, c_k, c_v, c_r = 0, GLA_QK, 2 * GLA_QK, 2 * GLA_QK + GLA_V
    c_qm = c_r + GLA_V
    c_a = c_qm + MEM_W

    z = _dot(h_ref[:, c_a:c_a + GLA_GATE_PAD].astype(BF16), wg2_ref[...]) + bg_ref[...]
    lg = (jnp.minimum(z, 0.0) - jnp.log1p(jnp.exp(-jnp.abs(z)))) * (1.0 / GLA_GATE_NORM)
    bc = _chunk_cumsum(lg)
    n_chunk = MIX_TILE // GLA_CHUNK
    last = [bc[(c + 1) * GLA_CHUNK - 1:(c + 1) * GLA_CHUNK, :] for c in range(n_chunk)]
    bl = jnp.concatenate([jnp.broadcast_to(t, (GLA_CHUNK, GLA_QK)) for t in last], axis=0)
    q = h_ref[:, c_q:c_q + GLA_QK]
    k = h_ref[:, c_k:c_k + GLA_QK]
    qe = ((q * (GLA_DK ** -0.5)) * jnp.exp(bc)).astype(BF16)
    ke = (k * jnp.exp(-bc)).astype(BF16)
    kd = k * jnp.exp(bl - bc)
    dec_t = jnp.exp(jnp.concatenate(last, axis=0)).T

    row = lax.broadcasted_iota(jnp.int32, (GLA_CHUNK, GLA_CHUNK), 0)
    col = lax.broadcasted_iota(jnp.int32, (GLA_CHUNK, GLA_CHUNK), 1)
    causal = row >= col
    units = [(c, hd) for c in range(n_chunk) for hd in range(GLA_HEADS)]

    def rows_of(c):
        return slice(c * GLA_CHUNK, (c + 1) * GLA_CHUNK)

    def dk_of(hd):
        return slice(hd * GLA_DK, (hd + 1) * GLA_DK)

    def dv_of(hd):
        return slice(hd * GLA_DV, (hd + 1) * GLA_DV)

    vcs = {(c, hd): h_ref[rows_of(c), c_v + hd * GLA_DV:c_v + (hd + 1) * GLA_DV].astype(BF16) for c, hd in units}
    raw = {(c, hd): _dot_nt(qe[rows_of(c), dk_of(hd)], ke[rows_of(c), dk_of(hd)]) for c, hd in units}
    upd = {(c, hd): _dot(kd[rows_of(c), dk_of(hd)].T.astype(BF16), vcs[c, hd]) for c, hd in units}
    att = {u: jnp.where(causal, raw[u], 0.0).astype(BF16) for u in units}
    intra = {u: _dot(att[u], vcs[u]) for u in units}
    state = [state_ref[hd] for hd in range(GLA_HEADS)]
    for c in range(n_chunk):
        for hd in range(GLA_HEADS):
            oc_ref[rows_of(c), dv_of(hd)] = intra[c, hd] + _dot(qe[rows_of(c), dk_of(hd)], state[hd].astype(BF16))
            state[hd] = state[hd] * dec_t[dk_of(hd), c:c + 1] + upd[c, hd]
    for hd in range(GLA_HEADS):
        state_ref[hd] = state[hd]

    gon = gon_ref[...]
    parts = []
    for hd in range(GLA_HEADS):
        o = oc_ref[:, dv_of(hd)]
        o = o * lax.rsqrt(jnp.mean(o * o, axis=-1, keepdims=True) + RMS_EPS) * gon
        r = h_ref[:, c_r + hd * GLA_DV:c_r + (hd + 1) * GLA_DV]
        parts.append((o * _silu(r)).astype(BF16))
    parts += _mem_attention(h_ref[:, c_qm:c_qm + MEM_W], kv_ref)
    y = _dot(jnp.concatenate(parts, axis=-1), wout_ref[0])
    o_ref[0] = _layer_norm(ALPHA * x + y, lng_ref[...], lnb_ref[...])


def _gla_layer(x, kv, depth, w_in, w_gate2, b_gate, g_onorm, wout, layer, ln_g, ln_b):
    b, s, _ = x.shape
    assert s % MIX_TILE == 0 and MIX_TILE // GLA_CHUNK == 8
    n_main = 2 * GLA_QK + 2 * GLA_V
    win = jnp.concatenate(
        [w_in[:, :n_main], w_in[:, n_main + GLA_GATE_RANK:],
         jnp.pad(w_in[:, n_main:n_main + GLA_GATE_RANK], ((0, 0), (0, GLA_GATE_PAD - GLA_GATE_RANK)))],
        axis=1).astype(BF16)
    wg2 = jnp.pad(w_gate2, ((0, GLA_GATE_PAD - GLA_GATE_RANK), (0, 0))).astype(BF16)
    return pl.pallas_call(
        _gla_kernel,
        grid=(b, s // MIX_TILE),
        in_specs=[
            pl.BlockSpec((1, MIX_TILE, D_MODEL), lambda i, j: (i, j, 0)),
            _resident(win.shape),
            _resident(wg2.shape),
            _resident((1, GLA_QK)),
            _resident((1, GLA_DV)),
            pl.BlockSpec((1, 1, N_MEM, 2 * MEM_W), lambda i, j: (depth, i, 0, 0)),
            _layer_slab(wout.shape, layer),
            _resident((1, D_MODEL)),
            _resident((1, D_MODEL)),
        ],
        out_specs=pl.BlockSpec((1, MIX_TILE, D_MODEL), lambda i, j: (i, j, 0)),
        out_shape=jax.ShapeDtypeStruct((b, s, D_MODEL), F32),
        scratch_shapes=[
            pltpu.VMEM((GLA_HEADS, GLA_DK, GLA_DV), F32),
            pltpu.VMEM((MIX_TILE, GLA_IN_PAD), F32),
            pltpu.VMEM((MIX_TILE, GLA_V), F32),
        ],
        compiler_params=pltpu.CompilerParams(
            dimension_semantics=("arbitrary", "arbitrary"), vmem_limit_bytes=VMEM_LIMIT),
        name="gla_mixer",
    )(x, win, wg2, b_gate.reshape(1, GLA_QK), g_onorm.reshape(1, GLA_DV), kv,
      wout, ln_g.reshape(1, D_MODEL), ln_b.reshape(1, D_MODEL))


CONV_SEG = MIX_TILE // 8
CONV_HALO_V = 32
CONV_HALO_ROWS = CONV_HALO_V * 8
CONV_GROUP = 256
CONV_NGROUP = CONV_CH // CONV_GROUP
CONV_NSLAB = CONV_CH // LANES


def _conv_kernel(x_ref, win_ref, w16_ref, bdw_ref, cg_ref, cb_ref, kv_ref, wout_ref, lng_ref, lnb_ref,
                 o_ref, ext_ref, exta_ref, extb_ref, cvp_ref, cv_ref):
    @pl.when(pl.program_id(1) == 0)
    def _():
        ext_ref[:, 0:CONV_HALO_ROWS, :] = jnp.zeros((CONV_NSLAB, CONV_HALO_ROWS, LANES), F32)

    x = x_ref[0]
    xb = x.astype(BF16)
    off = CONV_HALO_V - (CONV_WIDTH - 1)
    ext_rows = CONV_HALO_ROWS + MIX_TILE
    slabs_per_group = CONV_GROUP // LANES

    def project(g):
        lo = g * CONV_GROUP
        value = _dot(xb, win_ref[0, :, lo:lo + CONV_GROUP])
        gate = _dot(xb, win_ref[0, :, CONV_CH + lo:CONV_CH + lo + CONV_GROUP])
        u = value * jax.nn.sigmoid(gate)
        for half in range(slabs_per_group):
            sl = g * slabs_per_group + half
            us = u[:, half * LANES:(half + 1) * LANES]
            for s in range(8):
                seg = us[s * CONV_SEG:(s + 1) * CONV_SEG, :]
                ext_ref[sl, pl.ds(CONV_HALO_ROWS + s, CONV_SEG, stride=8), :] = seg
                if s < 7:
                    ext_ref[sl, pl.ds(s + 1, CONV_HALO_V, stride=8), :] = seg[CONV_SEG - CONV_HALO_V:, :]
            exta_ref[sl] = ext_ref[sl].astype(BF16)
            extb_ref[sl, 0:ext_rows - 16, :] = ext_ref[sl, 8:ext_rows - 8, :].astype(BF16)
            ext_ref[sl, pl.ds(0, CONV_HALO_V, stride=8), :] = us[MIX_TILE - CONV_HALO_V:, :]

    def convolve(g):
        for half in range(slabs_per_group):
            sl = g * slabs_per_group + half
            cols = slice(sl * LANES, (sl + 1) * LANES)
            bias = bdw_ref[:, cols]
            for k in range(MIX_TILE // 16):
                acc = None
                for j in range(CONV_WIDTH):
                    m = 2 * k + j + off
                    src = exta_ref if m % 2 == 0 else extb_ref
                    r0 = (m // 2) * 16
                    term = src[sl, r0:r0 + 16, :].astype(F32) * w16_ref[j, sl].astype(F32)
                    acc = term if acc is None else acc + term
                cvp_ref[sl, k * 16:(k + 1) * 16, :] = acc + bias
            for s in range(8):
                cv_ref[s * CONV_SEG:(s + 1) * CONV_SEG, cols] = cvp_ref[sl, pl.ds(s, CONV_SEG, stride=8), :]

    project(0)
    for g in range(1, CONV_NGROUP):
        project(g)
        convolve(g - 1)
    mem = _mem_attention(_dot(xb, win_ref[0, :, 2 * CONV_CH:]), kv_ref)
    convolve(CONV_NGROUP - 1)

    y = _silu(_layer_norm(cv_ref[...], cg_ref[...], cb_ref[...])).astype(BF16)
    out = _dot(jnp.concatenate([y] + mem, axis=-1), wout_ref[0])
    o_ref[0] = _layer_norm(ALPHA * x + out, lng_ref[...], lnb_ref[...])


def _conv_layer(x, kv, depth, win, w_dw, b_dw, c_g, c_b, wout, layer, ln_g, ln_b):
    b, s, _ = x.shape
    w16 = jnp.broadcast_to(w_dw.astype(BF16).reshape(CONV_WIDTH, CONV_NSLAB, 1, LANES),
                           (CONV_WIDTH, CONV_NSLAB, 16, LANES))
    return pl.pallas_call(
        _conv_kernel,
        grid=(b, s // MIX_TILE),
        in_specs=[
            pl.BlockSpec((1, MIX_TILE, D_MODEL), lambda i, j: (i, j, 0)),
            _layer_slab(win.shape, layer),
            _resident((CONV_WIDTH, CONV_NSLAB, 16, LANES)),
            _resident((1, CONV_CH)),
            _resident((1, CONV_CH)),
            _resident((1, CONV_CH)),
            pl.BlockSpec((1, 1, N_MEM, 2 * MEM_W), lambda i, j: (depth, i, 0, 0)),
            _layer_slab(wout.shape, layer),
            _resident((1, D_MODEL)),
            _resident((1, D_MODEL)),
        ],
        out_specs=pl.BlockSpec((1, MIX_TILE, D_MODEL), lambda i, j: (i, j, 0)),
        out_shape=jax.ShapeDtypeStruct((b, s, D_MODEL), F32),
        scratch_shapes=[
            pltpu.VMEM((CONV_NSLAB, CONV_HALO_ROWS + MIX_TILE, LANES), F32),
            pltpu.VMEM((CONV_NSLAB, CONV_HALO_ROWS + MIX_TILE, LANES), BF16),
            pltpu.VMEM((CONV_NSLAB, CONV_HALO_ROWS + MIX_TILE, LANES), BF16),
            pltpu.VMEM((CONV_NSLAB, MIX_TILE, LANES), F32),
            pltpu.VMEM((MIX_TILE, CONV_CH), F32),
        ],
        compiler_params=pltpu.CompilerParams(
            dimension_semantics=("arbitrary", "arbitrary"), vmem_limit_bytes=VMEM_LIMIT),
        name="conv_mixer",
    )(x, win, w16, b_dw.reshape(1, CONV_CH), c_g.reshape(1, CONV_CH),
      c_b.reshape(1, CONV_CH), kv, wout, ln_g.reshape(1, D_MODEL), ln_b.reshape(1, D_MODEL))


def kernel(x, mem, ln_g, ln_b, ffn1_w_gu, ffn1_w_down, ffn2_w_gu, ffn2_w_down, mem_w_kv, gla_w_in, gla_w_gate2, gla_b_gate, gla_g_onorm, gla_w_out, conv_w_in, conv_w_dw, conv_b_dw, conv_ln_g, conv_ln_b, conv_w_out):
    b, s, d = x.shape
    kv = _mem_kv(mem, mem_w_kv)
    ffn_w = [(ffn1_w_gu, ffn1_w_down), (ffn2_w_gu, ffn2_w_down)]
    gla_wout = gla_w_out.astype(BF16)
    conv_win = conv_w_in.astype(BF16)
    conv_wout = conv_w_out.astype(BF16)
    for i in range(DEPTH):
        j = i // 2
        x = _ffn_ln(x.reshape(b * s, d), ffn_w[0][0], ffn_w[0][1], i, ln_g[i, 0], ln_b[i, 0]).reshape(b, s, d)
        if i % 2 == 0:
            x = _gla_layer(x, kv, i, gla_w_in[j], gla_w_gate2[j], gla_b_gate[j], gla_g_onorm[j],
                           gla_wout, j, ln_g[i, 1], ln_b[i, 1])
        else:
            x = _conv_layer(x, kv, i, conv_win, conv_w_dw[j], conv_b_dw[j], conv_ln_g[j], conv_ln_b[j],
                            conv_wout, j, ln_g[i, 1], ln_b[i, 1])
        x = _ffn_ln(x.reshape(b * s, d), ffn_w[1][0], ffn_w[1][1], i, ln_g[i, 2], ln_b[i, 2]).reshape(b, s, d)
    return x
```
